```python
import jax, jax.numpy as jnp
from jax import lax
import numpy as np

D_MODEL = 4096
BATCH = 1
SEQ = 16384
DEPTH = 4

CTX_LEN = 256
GRID_W = 64
EPS = 1e-6
NEG_INF = -1e30
N_MOD = 6
ADA_RANK = 1024

NA_HEADS = 16
NA_HEAD_DIM = 128
NA_WIDTH = NA_HEADS * NA_HEAD_DIM
NA_WIN_H = 8
NA_WIN_W = 16
NA_SCALE = NA_HEAD_DIM ** -0.5

MLA_HEADS = 16
MLA_Q_RANK = 1536
MLA_KV_RANK = 512
MLA_NOPE_DIM = 128
MLA_ROPE_DIM = 64
MLA_QK_DIM = MLA_NOPE_DIM + MLA_ROPE_DIM
MLA_V_DIM = 128
MLA_WIDTH = MLA_HEADS * MLA_V_DIM
MLA_SCALE = MLA_QK_DIM ** -0.5
ROPE_THETA = 10000.0
Q_BLOCK = 128

D_FF = 6144
CONV_W = 3

IN_SPLIT_WIDTHS = (NA_WIDTH, NA_WIDTH, NA_WIDTH, MLA_Q_RANK, MLA_KV_RANK, MLA_ROPE_DIM, D_MODEL, D_MODEL)
IN_COLS = sum(IN_SPLIT_WIDTHS)

kernel_name = "hybrid_na_mla_dit_trunk"


def rms_norm(x, g):
    xf = x.astype(jnp.float32)
    y = xf * lax.rsqrt(jnp.mean(xf * xf, axis=-1, keepdims=True) + EPS)
    return (y * g.astype(jnp.float32)).astype(x.dtype)


def modulate(h, shift, scale):
    return h * (1.0 + scale) + shift


def adaln(cond, down, up, bias):
    m = (jax.nn.silu(cond) @ down) @ up + bias
    return jnp.split(m, N_MOD, axis=-1)


def rope_1d(t, pos):
    d = t.shape[-1]
    inv_freq = ROPE_THETA ** (-jnp.arange(0, d, 2, dtype=jnp.float32) / d)
    ang = pos.astype(jnp.float32)[:, None] * inv_freq[None, :]
    cos = jnp.cos(ang)[:, None, :]
    sin = jnp.sin(ang)[:, None, :]
    t1, t2 = jnp.split(t.astype(jnp.float32), 2, axis=-1)
    return jnp.concatenate([t1 * cos - t2 * sin, t1 * sin + t2 * cos], axis=-1).astype(t.dtype)


def rope_2d_tail(t, rows, cols):
    nope, rp = jnp.split(t, [MLA_NOPE_DIM], axis=-1)
    half = MLA_ROPE_DIM // 2
    return jnp.concatenate([nope, rope_1d(rp[..., :half], rows), rope_1d(rp[..., half:], cols)], axis=-1)


def qk_scores(q, k, scale):
    return jnp.einsum("bqhd,bkhd->bhqk", q, k).astype(jnp.float32) * scale


def attend(scores, values):
    p = jax.nn.softmax(jnp.concatenate(scores, axis=-1), axis=-1)
    out, off = None, 0
    for s_i, v_i in zip(scores, values):
        k_i = s_i.shape[-1]
        o_i = jnp.einsum("bhqk,bkhd->bqhd", p[..., off:off + k_i].astype(v_i.dtype), v_i)
        out = o_i if out is None else out + o_i
        off += k_i
    return out


def dense_attend(q, k, v, scale):
    b, n = q.shape[:2]
    return attend([qk_scores(q, k, scale)], [v]).reshape(b, n, -1)


def project_tokens(h, w_in, na_qn, na_kn, cq_norm, ckv_norm, w_q_up, w_kv_up, mla_qn, mla_kn, pos):
    b, n, _ = h.shape
    offsets = [int(o) for o in np.cumsum(IN_SPLIT_WIDTHS)[:-1]]
    a_q, a_k, a_v, cq, ckv, k_rope, g_na, g_mla = jnp.split(h @ w_in, offsets, axis=-1)
    na_q = rms_norm(a_q.reshape(b, n, NA_HEADS, NA_HEAD_DIM), na_qn)
    na_k = rms_norm(a_k.reshape(b, n, NA_HEADS, NA_HEAD_DIM), na_kn)
    na_v = a_v.reshape(b, n, NA_HEADS, NA_HEAD_DIM)
    mla_q = (rms_norm(cq, cq_norm) @ w_q_up).reshape(b, n, MLA_HEADS, MLA_QK_DIM)
    kv = (rms_norm(ckv, ckv_norm) @ w_kv_up).reshape(b, n, MLA_HEADS, MLA_NOPE_DIM + MLA_V_DIM)
    k_nope, mla_v = jnp.split(kv, [MLA_NOPE_DIM], axis=-1)
    k_rope = jnp.broadcast_to(k_rope[:, :, None, :], (b, n, MLA_HEADS, MLA_ROPE_DIM))
    mla_q = rms_norm(mla_q, mla_qn)
    mla_k = rms_norm(jnp.concatenate([k_nope, k_rope], axis=-1), mla_kn)
    if pos is not None:
        rows, cols = pos
        mla_q = rope_2d_tail(mla_q, rows, cols)
        mla_k = rope_2d_tail(mla_k, rows, cols)
    return (na_q, na_k, na_v, mla_q, mla_k, mla_v, jax.nn.sigmoid(g_na), jax.nn.sigmoid(g_mla))


def na_latent(q, k, v, k_ctx, v_ctx, rpb):
    b, n = q.shape[:2]
    rows_n = n // GRID_W
    kr = min(NA_WIN_H, rows_n)
    shp = (b, rows_n, GRID_W, NA_HEADS, NA_HEAD_DIM)
    kg, vg = k.reshape(shp), v.reshape(shp)
    q_rows = jnp.moveaxis(q.reshape(shp), 1, 0)
    qc = jnp.arange(GRID_W)
    kc = jnp.arange(GRID_W)
    cs = jnp.clip(qc - NA_WIN_W // 2, 0, GRID_W - NA_WIN_W)
    col_ok = (kc[None, :] >= cs[:, None]) & (kc[None, :] < cs[:, None] + NA_WIN_W)
    mask = jnp.broadcast_to(col_ok[:, None, :], (GRID_W, kr, GRID_W)).reshape(GRID_W, kr * GRID_W)
    col_idx = jnp.clip(kc[None, :] - qc[:, None] + NA_WIN_W - 1, 0, 2 * NA_WIN_W - 2)
    rpb_cols = rpb.astype(jnp.float32)[:, :, col_idx]

    def row_fn(args):
        r, q_r = args
        start = jnp.clip(r - kr // 2, 0, rows_n - kr)
        k_blk = lax.dynamic_slice_in_dim(kg, start, kr, axis=1).reshape(b, kr * GRID_W, NA_HEADS, NA_HEAD_DIM)
        v_blk = lax.dynamic_slice_in_dim(vg, start, kr, axis=1).reshape(b, kr * GRID_W, NA_HEADS, NA_HEAD_DIM)
        bias = lax.dynamic_slice_in_dim(rpb_cols, start - r + NA_WIN_H - 1, kr, axis=1)
        bias = bias.transpose(0, 2, 1, 3).reshape(NA_HEADS, GRID_W, kr * GRID_W)
        s_win = jnp.where(mask, qk_scores(q_r, k_blk, NA_SCALE) + bias, NEG_INF)
        s_ctx = qk_scores(q_r, k_ctx, NA_SCALE)
        return attend([s_win, s_ctx], [v_blk, v_ctx])

    out = lax.map(row_fn, (jnp.arange(rows_n, dtype=jnp.int32), q_rows))
    return jnp.moveaxis(out, 0, 1).reshape(b, n, NA_WIDTH)


def mla_latent(q, k, v, k_ctx, v_ctx):
    b, n = q.shape[:2]
    nb = n // Q_BLOCK
    q_blocks = jnp.moveaxis(q.reshape(b, nb, Q_BLOCK, MLA_HEADS, MLA_QK_DIM), 1, 0)

    def blk_fn(q_b):
        return attend([qk_scores(q_b, k, MLA_SCALE), qk_scores(q_b, k_ctx, MLA_SCALE)], [v, v_ctx])

    out = lax.map(blk_fn, q_blocks)
    return jnp.moveaxis(out, 0, 1).reshape(b, n, MLA_WIDTH)


def merge_branches(y_na, y_mla, g_na, g_mla, w_b_na, w_b_mla, w_o):
    return (g_na * (y_na @ w_b_na) + g_mla * (y_mla @ w_b_mla)) @ w_o


def dwconv_seq(t, w, bias):
    y = lax.conv_general_dilated(t, w[:, None, :], window_strides=(1,),
                                 padding=[(CONV_W // 2, CONV_W // 2)],
                                 dimension_numbers=("NWC", "WIO", "NWC"),
                                 feature_group_count=t.shape[-1])
    return y + bias


def conv_ffn(h, w_up, conv_w, conv_b, w_down):
    gate, val = jnp.split(h @ w_up, 2, axis=-1)
    return (jax.nn.silu(dwconv_seq(gate, conv_w, conv_b)) * val) @ w_down


def _normal(k, shape, scale):
    return jax.random.normal(k, shape, jnp.float32) * scale


def setup_inputs(seed: int = 0) -> dict:
    key = jax.random.key(seed)
    ks = jax.random.split(key, 26)
    L = DEPTH
    return {
        "x": _normal(ks[0], (BATCH, SEQ, D_MODEL), 1.0),
        "c": _normal(ks[1], (BATCH, D_MODEL), 1.0),
        "ctx": _normal(ks[2], (BATCH, CTX_LEN, D_MODEL), 1.0),
        "c_ctx": _normal(ks[3], (D_MODEL,), 1.0),
        "ada_down": _normal(ks[4], (L, D_MODEL, ADA_RANK), D_MODEL ** -0.5),
        "ada_up": _normal(ks[5], (L, ADA_RANK, N_MOD * D_MODEL), 0.3 * ADA_RANK ** -0.5),
        "ada_bias": _normal(ks[6], (L, N_MOD * D_MODEL), 0.01),
        "norm_mix": 1.0 + _normal(ks[7], (L, D_MODEL), 0.02),
        "norm_ffn": 1.0 + _normal(ks[8], (L, D_MODEL), 0.02),
        "w_in": _normal(ks[9], (L, D_MODEL, IN_COLS), D_MODEL ** -0.5),
        "na_q_norm": 1.0 + _normal(ks[10], (L, NA_HEAD_DIM), 0.02),
        "na_k_norm": 1.0 + _normal(ks[11], (L, NA_HEAD_DIM), 0.02),
        "na_rpb": _normal(ks[12], (L, NA_HEADS, 2 * NA_WIN_H - 1, 2 * NA_WIN_W - 1), 0.1),
        "mla_cq_norm": 1.0 + _normal(ks[13], (L, MLA_Q_RANK), 0.02),
        "mla_ckv_norm": 1.0 + _normal(ks[14], (L, MLA_KV_RANK), 0.02),
        "mla_w_q_up": _normal(ks[15], (L, MLA_Q_RANK, MLA_HEADS * MLA_QK_DIM), MLA_Q_RANK ** -0.5),
        "mla_w_kv_up": _normal(ks[16], (L, MLA_KV_RANK, MLA_HEADS * (MLA_NOPE_DIM + MLA_V_DIM)), MLA_KV_RANK ** -0.5),
        "mla_q_norm": 1.0 + _normal(ks[17], (L, MLA_QK_DIM), 0.02),
        "mla_k_norm": 1.0 + _normal(ks[18], (L, MLA_QK_DIM), 0.02),
        "w_branch_na": _normal(ks[19], (L, NA_WIDTH, D_MODEL), NA_WIDTH ** -0.5),
        "w_branch_mla": _normal(ks[20], (L, MLA_WIDTH, D_MODEL), MLA_WIDTH ** -0.5),
        "w_out": _normal(ks[21], (L, D_MODEL, D_MODEL), D_MODEL ** -0.5),
        "ffn_w_up": _normal(ks[22], (L, D_MODEL, 2 * D_FF), D_MODEL ** -0.5),
        "ffn_conv_w": _normal(ks[23], (L, CONV_W, D_FF), CONV_W ** -0.5),
        "ffn_conv_b": _normal(ks[24], (L, D_FF), 0.01),
        "ffn_w_down": _normal(ks[25], (L, D_FF, D_MODEL), D_FF ** -0.5),
    }


def reference(x, c, ctx, c_ctx, ada_down, ada_up, ada_bias, norm_mix, norm_ffn, w_in,
              na_q_norm, na_k_norm, na_rpb, mla_cq_norm, mla_ckv_norm, mla_w_q_up, mla_w_kv_up,
              mla_q_norm, mla_k_norm, w_branch_na, w_branch_mla, w_out,
              ffn_w_up, ffn_conv_w, ffn_conv_b, ffn_w_down):
    n_tok = x.shape[1]
    t = jnp.arange(n_tok, dtype=jnp.int32)
    pos = (t // GRID_W, t % GRID_W)
    ctx_s = ctx
    for l in range(DEPTH):
        last = l == DEPTH - 1
        sh1, sc1, g1, sh2, sc2, g2 = [m[:, None, :] for m in adaln(c, ada_down[l], ada_up[l], ada_bias[l])]
        csh1, csc1, cg1, csh2, csc2, cg2 = adaln(c_ctx, ada_down[l], ada_up[l], ada_bias[l])
        proj = (w_in[l], na_q_norm[l], na_k_norm[l], mla_cq_norm[l], mla_ckv_norm[l],
                mla_w_q_up[l], mla_w_kv_up[l], mla_q_norm[l], mla_k_norm[l])
        merge_w = (w_branch_na[l], w_branch_mla[l], w_out[l])
        ffn_w = (ffn_w_up[l], ffn_conv_w[l], ffn_conv_b[l], ffn_w_down[l])

        h = modulate(rms_norm(x, norm_mix[l]), sh1, sc1)
        hc = modulate(rms_norm(ctx_s, norm_mix[l]), csh1, csc1)
        nq, nk, nv, mq, mk, mv, gn, gm = project_tokens(h, *proj, pos=pos)
        cnq, cnk, cnv, cmq, cmk, cmv, cgn, cgm = project_tokens(hc, *proj, pos=None)
        y_na = na_latent(nq, nk, nv, cnk, cnv, na_rpb[l])
        y_mla = mla_latent(mq, mk, mv, cmk, cmv)
        x = x + g1 * merge_branches(y_na, y_mla, gn, gm, *merge_w)
        x = x + g2 * conv_ffn(modulate(rms_norm(x, norm_ffn[l]), sh2, sc2), *ffn_w)

        if not last:
            yc_na = dense_attend(cnq, cnk, cnv, NA_SCALE)
            yc_mla = dense_attend(cmq, cmk, cmv, MLA_SCALE)
            ctx_s = ctx_s + cg1 * merge_branches(yc_na, yc_mla, cgn, cgm, *merge_w)
            ctx_s = ctx_s + cg2 * conv_ffn(modulate(rms_norm(ctx_s, norm_ffn[l]), csh2, csc2), *ffn_w)
    return x
```

```python
import functools

import numpy as np
import jax
import jax.numpy as jnp
from jax import lax
from jax.experimental import pallas as pl
from jax.experimental.pallas import tpu as pltpu

F32 = jnp.float32
BF16 = jnp.bfloat16

GRID_W = 64
EPS = 1e-6
NEG_INF = -1e30
N_MOD = 6

NA_HEADS = 16
NA_HEAD_DIM = 128
NA_WIN_H = 8
NA_WIN_W = 16

MLA_HEADS = 16
MLA_NOPE_DIM = 128
MLA_ROPE_DIM = 64
MLA_QK_DIM = MLA_NOPE_DIM + MLA_ROPE_DIM
MLA_V_DIM = 128
ROPE_THETA = 10000.0
CONV_W = 3

LANES = 128
MLA_QK_PAD = 2 * LANES
VMEM_LIMIT_BYTES = 56 * 1024 * 1024


def _tile(dim, pref, mult=8):
    if dim <= pref:
        return dim
    t = (pref // mult) * mult
    while t >= mult:
        if dim % t == 0:
            return t
        t -= mult
    return dim


def _params(*sem):
    return pltpu.CompilerParams(dimension_semantics=sem, vmem_limit_bytes=VMEM_LIMIT_BYTES)


def _ada_down_kernel(c_ref, w_ref, o_ref):
    c = c_ref[...]
    a = (c * jax.nn.sigmoid(c)).astype(BF16)
    o_ref[...] = jnp.dot(a, w_ref[...].astype(BF16), preferred_element_type=F32)


def _ada_up_kernel(t_ref, w_ref, b_ref, o_ref):
    o_ref[...] = jnp.dot(t_ref[...].astype(BF16), w_ref[...].astype(BF16),
                         preferred_element_type=F32) + b_ref[...]


def _adaln(cond, down, up, bias):
    rows, d = cond.shape
    rank = down.shape[1]
    n = up.shape[1]
    bn1 = _tile(rank, 512, LANES)
    t = pl.pallas_call(
        _ada_down_kernel,
        grid=(rank // bn1,),
        in_specs=[pl.BlockSpec((rows, d), lambda j: (0, 0)),
                  pl.BlockSpec((d, bn1), lambda j: (0, j))],
        out_specs=pl.BlockSpec((rows, bn1), lambda j: (0, j)),
        out_shape=jax.ShapeDtypeStruct((rows, rank), F32),
        compiler_params=_params("parallel"),
        name="ada_down",
    )(cond, down)
    bn2 = _tile(n, 2048, LANES)
    return pl.pallas_call(
        _ada_up_kernel,
        grid=(n // bn2,),
        in_specs=[pl.BlockSpec((rows, rank), lambda j: (0, 0)),
                  pl.BlockSpec((rank, bn2), lambda j: (0, j)),
                  pl.BlockSpec((1, bn2), lambda j: (0, j))],
        out_specs=pl.BlockSpec((rows, bn2), lambda j: (0, j)),
        out_shape=jax.ShapeDtypeStruct((rows, n), F32),
        compiler_params=_params("parallel"),
        name="ada_up",
    )(t, up, bias.reshape(1, n))


def _norm_mod_kernel(x_ref, g_ref, sc_ref, sh_ref, o_ref):
    x = x_ref[...]
    y = x * lax.rsqrt(jnp.mean(x * x, axis=-1, keepdims=True) + EPS) * g_ref[...]
    o_ref[...] = (y * (1.0 + sc_ref[...]) + sh_ref[...]).astype(o_ref.dtype)


def _norm_mod(x, g, sc, sh):
    m, d = x.shape
    bm = _tile(m, 256)
    vec = pl.BlockSpec((1, d), lambda i: (0, 0))
    return pl.pallas_call(
        _norm_mod_kernel,
        grid=(m // bm,),
        in_specs=[pl.BlockSpec((bm, d), lambda i: (i, 0)), vec, vec, vec],
        out_specs=pl.BlockSpec((bm, d), lambda i: (i, 0)),
        out_shape=jax.ShapeDtypeStruct((m, d), BF16),
        compiler_params=_params("parallel"),
        name="norm_mod",
    )(x, g.reshape(1, d), sc.reshape(1, d), sh.reshape(1, d))


def _mm_kernel(x_ref, w_ref, *rest, epilogue):
    acc = jnp.dot(x_ref[...], w_ref[...], preferred_element_type=F32)
    epilogue(acc, *rest)


def _matmul(x, w, epilogue, extras, extra_specs, out_shapes, out_specs, bm, bn, name):
    m, k = x.shape
    n = w.shape[1]
    return pl.pallas_call(
        functools.partial(_mm_kernel, epilogue=epilogue),
        grid=(m // bm, n // bn),
        in_specs=[pl.BlockSpec((bm, k), lambda i, j: (i, 0)),
                  pl.BlockSpec((k, bn), lambda i, j: (0, j))] + list(extra_specs),
        out_specs=out_specs,
        out_shape=out_shapes,
        compiler_params=_params("parallel", "parallel"),
        name=name,
    )(x, w, *extras)


def _epi_cast(acc, o_ref):
    o_ref[...] = acc.astype(o_ref.dtype)


def _epi_headnorm(acc, g_ref, o_ref):
    for c in range(acc.shape[1] // LANES):
        sl = slice(c * LANES, (c + 1) * LANES)
        a = acc[:, sl]
        r = lax.rsqrt(jnp.mean(a * a, axis=-1, keepdims=True) + EPS)
        o_ref[:, sl] = (a * r * g_ref[:, sl]).astype(o_ref.dtype)


def _epi_rownorm(acc, g_ref, o_ref):
    r = lax.rsqrt(jnp.mean(acc * acc, axis=-1, keepdims=True) + EPS)
    o_ref[...] = (acc * r * g_ref[...]).astype(o_ref.dtype)


def _epi_ckv(acc, g_ref, ckv_ref, kr_ref, *, kv_rank):
    a = acc[:, :kv_rank]
    r = lax.rsqrt(jnp.mean(a * a, axis=-1, keepdims=True) + EPS)
    ckv_ref[...] = (a * r * g_ref[...]).astype(ckv_ref.dtype)
    kr_ref[...] = acc[:, kv_rank:]


def _epi_q_up(acc, c_ref, s_ref, gn_ref, gr_ref, gs_ref, o_ref, *, heads, scale):
    cg = c_ref[...] * gr_ref[...]
    sg = s_ref[...] * gs_ref[...]
    for h in range(heads):
        base = h * 3 * LANES
        nope = acc[:, base:base + LANES]
        rp = acc[:, base + LANES:base + 2 * LANES]
        sw = acc[:, base + 2 * LANES:base + 3 * LANES]
        ss = jnp.sum(nope * nope, axis=-1, keepdims=True) + jnp.sum(rp * rp, axis=-1, keepdims=True)
        r = lax.rsqrt(ss * (1.0 / MLA_QK_DIM) + EPS) * scale
        o_ref[h, :, 0:LANES] = (nope * r * gn_ref[...]).astype(o_ref.dtype)
        o_ref[h, :, LANES:2 * LANES] = ((rp * cg + sw * sg) * r).astype(o_ref.dtype)


def _epi_kv_up(acc, kr_ref, c_ref, s_ref, gn_ref, gr_ref, gs_ref, k_ref, v_ref, *, heads):
    kr = kr_ref[:, 0:LANES]
    krs = kr_ref[:, LANES:2 * LANES]
    ssr = jnp.sum(kr * kr, axis=-1, keepdims=True)
    rope = kr * (c_ref[...] * gr_ref[...]) + krs * (s_ref[...] * gs_ref[...])
    for h in range(heads):
        base = h * 2 * LANES
        nope = acc[:, base:base + LANES]
        ss = jnp.sum(nope * nope, axis=-1, keepdims=True) + ssr
        r = lax.rsqrt(ss * (1.0 / MLA_QK_DIM) + EPS)
        k_ref[h, :, 0:LANES] = (nope * r * gn_ref[...]).astype(k_ref.dtype)
        k_ref[h, :, LANES:2 * LANES] = (rope * r).astype(k_ref.dtype)
        v_ref[h] = acc[:, base + LANES:base + 2 * LANES].astype(v_ref.dtype)


def _epi_residual(acc, x_ref, g_ref, o_ref):
    o_ref[...] = x_ref[...] + g_ref[...] * acc


def _project(h, lw, tabs):
    m, d = h.shape
    bm = _tile(m, 1024)
    naw = NA_HEADS * NA_HEAD_DIM
    row = lambda n: pl.BlockSpec((1, n), lambda i, j: (0, 0))

    bn = _tile(2 * naw, 1024, LANES)
    na_qk = _matmul(h, lw["w_na_qk"], _epi_headnorm, [lw["g_na_qk"]],
                    [pl.BlockSpec((1, bn), lambda i, j: (0, j))],
                    jax.ShapeDtypeStruct((m, 2 * naw), BF16),
                    pl.BlockSpec((bm, bn), lambda i, j: (i, j)), bm, bn, "na_qk_proj")
    bn = _tile(naw, 1024, LANES)
    na_v = _matmul(h, lw["w_na_v"], _epi_cast, [], [],
                   jax.ShapeDtypeStruct((m, naw), BF16),
                   pl.BlockSpec((bm, bn), lambda i, j: (i, j)), bm, bn, "na_v_proj")

    q_rank = lw["w_cq"].shape[1]
    kv_rank = lw["w_ckv_kr"].shape[1] - 2 * LANES
    bmc = _tile(m, 512)
    cqn = _matmul(h, lw["w_cq"], _epi_rownorm, [lw["g_cq"]], [row(q_rank)],
                  jax.ShapeDtypeStruct((m, q_rank), BF16),
                  pl.BlockSpec((bmc, q_rank), lambda i, j: (i, 0)), bmc, q_rank, "cq_proj")
    ckvn, kr = _matmul(h, lw["w_ckv_kr"], functools.partial(_epi_ckv, kv_rank=kv_rank),
                       [lw["g_ckv"]], [row(kv_rank)],
                       (jax.ShapeDtypeStruct((m, kv_rank), BF16),
                        jax.ShapeDtypeStruct((m, 2 * LANES), F32)),
                       (pl.BlockSpec((bmc, kv_rank), lambda i, j: (i, 0)),
                        pl.BlockSpec((bmc, 2 * LANES), lambda i, j: (i, 0))),
                       bmc, kv_rank + 2 * LANES, "ckv_proj")

    cos_t, sin_t = tabs
    hb = 4 if MLA_HEADS % 4 == 0 else 1
    bmq = _tile(m, 512)
    tab = pl.BlockSpec((bmq, LANES), lambda i, j: (i, 0))
    vec = pl.BlockSpec((1, LANES), lambda i, j: (0, 0))
    mla_q = _matmul(cqn, lw["w_q_up"],
                    functools.partial(_epi_q_up, heads=hb, scale=MLA_QK_DIM ** -0.5),
                    [cos_t, sin_t, lw["gq_nope"], lw["gq_rope"], lw["gq_swap"]],
                    [tab, tab, vec, vec, vec],
                    jax.ShapeDtypeStruct((MLA_HEADS, m, MLA_QK_PAD), BF16),
                    pl.BlockSpec((hb, bmq, MLA_QK_PAD), lambda i, j: (j, i, 0)),
                    bmq, hb * 3 * LANES, "mla_q_up")
    mla_k, mla_v = _matmul(ckvn, lw["w_kv_up"], functools.partial(_epi_kv_up, heads=hb),
                           [kr, cos_t, sin_t, lw["gk_nope"], lw["gk_rope"], lw["gk_swap"]],
                           [pl.BlockSpec((bmq, 2 * LANES), lambda i, j: (i, 0)), tab, tab, vec, vec, vec],
                           (jax.ShapeDtypeStruct((MLA_HEADS, m, MLA_QK_PAD), BF16),
                            jax.ShapeDtypeStruct((MLA_HEADS, m, MLA_V_DIM), BF16)),
                           (pl.BlockSpec((hb, bmq, MLA_QK_PAD), lambda i, j: (j, i, 0)),
                            pl.BlockSpec((hb, bmq, MLA_V_DIM), lambda i, j: (j, i, 0))),
                           bmq, hb * 2 * LANES, "mla_kv_up")
    return na_qk, na_v, mla_q, mla_k, mla_v


def _dot_nt(a, b):
    return lax.dot_general(a, b, (((1,), (1,)), ((), ())), preferred_element_type=F32)


def _flash_kernel(*refs, tk, n_chunks, has_ctx):
    if has_ctx:
        q_ref, k_ref, v_ref, kc_ref, vc_ref, o_ref = refs
    else:
        q_ref, k_ref, v_ref, o_ref = refs
    q = q_ref[...]
    tq, dv = q.shape[0], v_ref.shape[-1]

    if has_ctx:
        s = _dot_nt(q, kc_ref[...])
        m0 = jnp.max(s, axis=-1, keepdims=True)
        p = jnp.exp(s - m0)
        l0 = jnp.sum(p, axis=-1, keepdims=True)
        acc0 = jnp.dot(p.astype(BF16), vc_ref[...], preferred_element_type=F32)
    else:
        m0 = jnp.full((tq, 1), -jnp.inf, F32)
        l0 = jnp.zeros((tq, 1), F32)
        acc0 = jnp.zeros((tq, dv), F32)

    def body(c, carry):
        m, l, acc = carry
        off = pl.multiple_of(c * tk, tk)
        s = _dot_nt(q, k_ref[pl.ds(off, tk), :])
        m_new = jnp.maximum(m, jnp.max(s, axis=-1, keepdims=True))
        alpha = jnp.exp(m - m_new)
        p = jnp.exp(s - m_new)
        l = alpha * l + jnp.sum(p, axis=-1, keepdims=True)
        acc = alpha * acc + jnp.dot(p.astype(BF16), v_ref[pl.ds(off, tk), :],
                                    preferred_element_type=F32)
        return m_new, l, acc

    _, l, acc = lax.fori_loop(0, n_chunks, body, (m0, l0, acc0))
    o_ref[...] = (acc / l).astype(o_ref.dtype)


def _flash(q, k, v, ctx_kv, *, heads, head_major, name):
    if head_major:
        mq, dq = q.shape[1], q.shape[2]
        mk, dv = k.shape[1], v.shape[2]
    else:
        mq, dq = q.shape[0], q.shape[1] // heads
        mk, dv = k.shape[0], v.shape[1] // heads
    tq = _tile(mq, 512)
    tk = _tile(mk, 512)

    def spec(rows, d, tiled):
        if head_major:
            return pl.BlockSpec((None, rows, d), (lambda h, i: (h, i, 0)) if tiled else (lambda h, i: (h, 0, 0)))
        return pl.BlockSpec((rows, d), (lambda h, i: (i, h)) if tiled else (lambda h, i: (0, h)))

    in_specs = [spec(tq, dq, True), spec(mk, dq, False), spec(mk, dv, False)]
    args = [q, k, v]
    if ctx_kv is not None:
        kc, vc = ctx_kv
        mc = kc.shape[1] if head_major else kc.shape[0]
        in_specs += [spec(mc, dq, False), spec(mc, dv, False)]
        args += [kc, vc]
    return pl.pallas_call(
        functools.partial(_flash_kernel, tk=tk, n_chunks=mk // tk, has_ctx=ctx_kv is not None),
        grid=(heads, mq // tq),
        in_specs=in_specs,
        out_specs=pl.BlockSpec((tq, dv), lambda h, i: (i, h)),
        out_shape=jax.ShapeDtypeStruct((mq, heads * dv), BF16),
        compiler_params=_params("parallel", "parallel"),
        name=name,
    )(*args)


def _na_kernel(q_ref, k_ref, v_ref, kc_ref, vc_ref, b_ref, o_ref):
    for h in range(NA_HEADS):
        sl = slice(h * NA_HEAD_DIM, (h + 1) * NA_HEAD_DIM)
        q = q_ref[:, sl]
        s_w = _dot_nt(q, k_ref[:, sl]) + b_ref[0, h]
        s_c = _dot_nt(q, kc_ref[:, sl])
        m = jnp.maximum(jnp.max(s_w, axis=-1, keepdims=True), jnp.max(s_c, axis=-1, keepdims=True))
        p_w = jnp.exp(s_w - m)
        p_c = jnp.exp(s_c - m)
        l = jnp.sum(p_w, axis=-1, keepdims=True) + jnp.sum(p_c, axis=-1, keepdims=True)
        o = (jnp.dot(p_w.astype(BF16), v_ref[:, sl], preferred_element_type=F32)
             + jnp.dot(p_c.astype(BF16), vc_ref[:, sl], preferred_element_type=F32))
        o_ref[:, sl] = (o / l).astype(o_ref.dtype)


def _na_bias_table(rpb):
    qc = np.arange(GRID_W)
    kc = np.arange(GRID_W)
    cs = np.clip(qc - NA_WIN_W // 2, 0, GRID_W - NA_WIN_W)
    col_ok = (kc[None, :] >= cs[:, None]) & (kc[None, :] < cs[:, None] + NA_WIN_W)
    col_idx = np.clip(kc[None, :] - qc[:, None] + NA_WIN_W - 1, 0, 2 * NA_WIN_W - 2)
    cols = rpb.astype(F32)[:, :, col_idx]
    cols = jnp.where(jnp.asarray(col_ok)[None, None], cols, NEG_INF)
    variants = [cols[:, v:v + NA_WIN_H].transpose(0, 2, 1, 3).reshape(NA_HEADS, GRID_W, NA_WIN_H * GRID_W)
                for v in range(NA_WIN_H)]
    return jnp.stack(variants)


def _na_attention(na_qk, na_v, ctx_qk, ctx_v, bias):
    m = na_qk.shape[0]
    naw = NA_HEADS * NA_HEAD_DIM
    rows_n = m // GRID_W
    win = NA_WIN_H * GRID_W
    n_ctx = ctx_qk.shape[0]

    def start(r):
        return jnp.clip(r - NA_WIN_H // 2, 0, rows_n - NA_WIN_H)

    E = pl.Element
    return pl.pallas_call(
        _na_kernel,
        grid=(rows_n,),
        in_specs=[
            pl.BlockSpec((GRID_W, naw), lambda r: (r, 0)),
            pl.BlockSpec((E(win), E(naw)), lambda r: (start(r) * GRID_W, naw)),
            pl.BlockSpec((E(win), E(naw)), lambda r: (start(r) * GRID_W, 0)),
            pl.BlockSpec((n_ctx, naw), lambda r: (0, 1)),
            pl.BlockSpec((n_ctx, naw), lambda r: (0, 0)),
            pl.BlockSpec((1, NA_HEADS, GRID_W, win), lambda r: (start(r) - r + NA_WIN_H - 1, 0, 0, 0)),
        ],
        out_specs=pl.BlockSpec((GRID_W, naw), lambda r: (r, 0)),
        out_shape=jax.ShapeDtypeStruct((m, naw), BF16),
        compiler_params=_params("parallel"),
        name="na_attention",
    )(na_qk, na_qk, na_v, ctx_qk, ctx_v, bias)


def _merge_kernel(h_ref, yn_ref, ym_ref, wgn_ref, wgm_ref, wbn_ref, wbm_ref, o_ref):
    h = h_ref[...]
    gn = jax.nn.sigmoid(jnp.dot(h, wgn_ref[...], preferred_element_type=F32))
    gm = jax.nn.sigmoid(jnp.dot(h, wgm_ref[...], preferred_element_type=F32))
    a = jnp.dot(yn_ref[...], wbn_ref[...], preferred_element_type=F32)
    b = jnp.dot(ym_ref[...], wbm_ref[...], preferred_element_type=F32)
    o_ref[...] = (gn * a + gm * b).astype(o_ref.dtype)


def _merge(h, y_na, y_mla, lw):
    m, d = h.shape
    bm = _tile(m, 512)
    bn = _tile(d, 512, LANES)
    nj = d // bn
    kn, km = y_na.shape[1], y_mla.shape[1]
    return pl.pallas_call(
        _merge_kernel,
        grid=(m // bm, nj),
        in_specs=[pl.BlockSpec((bm, d), lambda i, j: (i, 0)),
                  pl.BlockSpec((bm, kn), lambda i, j: (i, 0)),
                  pl.BlockSpec((bm, km), lambda i, j: (i, 0)),
                  pl.BlockSpec((d, bn), lambda i, j: (0, j)),
                  pl.BlockSpec((d, bn), lambda i, j: (0, j + nj)),
                  pl.BlockSpec((kn, bn), lambda i, j: (0, j)),
                  pl.BlockSpec((km, bn), lambda i, j: (0, j))],
        out_specs=pl.BlockSpec((bm, bn), lambda i, j: (i, j)),
        out_shape=jax.ShapeDtypeStruct((m, d), BF16),
        compiler_params=_params("parallel", "parallel"),
        name="branch_merge",
    )(h, y_na, y_mla, lw["w_gates"], lw["w_gates"], lw["w_b_na"], lw["w_b_mla"])


def _residual_matmul(a, w, x, gate, name):
    m, d = x.shape
    bm = _tile(m, 512)
    bn = _tile(d, 1024, LANES)
    return _matmul(a, w, _epi_residual, [x, gate.reshape(1, d)],
                   [pl.BlockSpec((bm, bn), lambda i, j: (i, j)),
                    pl.BlockSpec((1, bn), lambda i, j: (0, j))],
                   jax.ShapeDtypeStruct((m, d), F32),
                   pl.BlockSpec((bm, bn), lambda i, j: (i, j)), bm, bn, name)


def _ffn_up_kernel(x_ref, wg_ref, wv_ref, halo_ref, cw_ref, cb_ref, o_ref):
    x = x_ref[...]
    g = jnp.dot(x, wg_ref[...], preferred_element_type=F32)
    bm = g.shape[0]
    rows = lax.broadcasted_iota(jnp.int32, g.shape, 0)
    g_prev = jnp.where(rows == 0, halo_ref[0:1, :], pltpu.roll(g, 1, axis=0))
    g_next = jnp.where(rows == bm - 1, halo_ref[1:2, :], pltpu.roll(g, bm - 1, axis=0))
    y = g_prev * cw_ref[0:1, :] + g * cw_ref[1:2, :] + g_next * cw_ref[2:3, :] + cb_ref[...]
    v = jnp.dot(x, wv_ref[...], preferred_element_type=F32)
    o_ref[...] = (y * jax.nn.sigmoid(y) * v).astype(o_ref.dtype)


def _ffn_up(h2, lw):
    m, d = h2.shape
    dff = lw["conv_b"].shape[1]
    bm = _tile(m, 1024)
    bn = _tile(dff, 512, LANES)
    nt, nj = m // bm, dff // bn

    edge = h2.reshape(nt, bm, d)[:, (0, bm - 1), :].reshape(2 * nt, d)
    pad = (-edge.shape[0]) % 16
    edge = jnp.pad(edge, ((0, pad), (0, 0)))
    bne = _tile(dff, 1024, LANES)
    eg = _matmul(edge, lw["w_up"][:, :dff], _epi_cast, [], [],
                 jax.ShapeDtypeStruct((edge.shape[0], dff), F32),
                 pl.BlockSpec((edge.shape[0], bne), lambda i, j: (i, j)),
                 edge.shape[0], bne, "ffn_edge_rows")
    eg = eg[:2 * nt].reshape(nt, 2, dff)
    zero = jnp.zeros((1, dff), F32)
    prev_rows = jnp.concatenate([zero, eg[:-1, 1]], axis=0)
    next_rows = jnp.concatenate([eg[1:, 0], zero], axis=0)
    halo = jnp.stack([prev_rows, next_rows], axis=1)

    return pl.pallas_call(
        _ffn_up_kernel,
        grid=(m // bm, nj),
        in_specs=[pl.BlockSpec((bm, d), lambda i, j: (i, 0)),
                  pl.BlockSpec((d, bn), lambda i, j: (0, j)),
                  pl.BlockSpec((d, bn), lambda i, j: (0, j + nj)),
                  pl.BlockSpec((None, 2, bn), lambda i, j: (i, 0, j)),
                  pl.BlockSpec((CONV_W, bn), lambda i, j: (0, j)),
                  pl.BlockSpec((1, bn), lambda i, j: (0, j))],
        out_specs=pl.BlockSpec((bm, bn), lambda i, j: (i, j)),
        out_shape=jax.ShapeDtypeStruct((m, dff), BF16),
        compiler_params=_params("parallel", "parallel"),
        name="ffn_up_conv",
    )(h2, lw["w_up"], lw["w_up"], halo, lw["conv_w"], lw["conv_b"])


def _rope_partner():
    half = MLA_ROPE_DIM // 2
    quarter = half // 2
    idx = np.arange(MLA_ROPE_DIM)
    return np.where((idx % half) < quarter, idx + quarter, idx - quarter)


def _pad_lanes(a):
    return jnp.pad(a, [(0, 0)] * (a.ndim - 1) + [(0, LANES - a.shape[-1])])


def _layer_weights(l, w_in, na_q_norm, na_k_norm, mla_cq_norm, mla_ckv_norm, mla_w_q_up, mla_w_kv_up,
                   mla_q_norm, mla_k_norm, w_branch_na, w_branch_mla, w_out,
                   ffn_w_up, ffn_conv_w, ffn_conv_b, ffn_w_down):
    naw = NA_HEADS * NA_HEAD_DIM
    q_rank = mla_w_q_up.shape[1]
    kv_rank = mla_w_kv_up.shape[1]
    perm = _rope_partner()
    wl = w_in[l]
    o_cq = 3 * naw
    o_ckv = o_cq + q_rank
    o_kr = o_ckv + kv_rank
    o_g = o_kr + MLA_ROPE_DIM
    w_kr = wl[:, o_kr:o_g]
    lw = {
        "w_na_qk": wl[:, :2 * naw].astype(BF16),
        "w_na_v": wl[:, 2 * naw:3 * naw].astype(BF16),
        "w_cq": wl[:, o_cq:o_ckv].astype(BF16),
        "w_ckv_kr": jnp.concatenate([wl[:, o_ckv:o_kr], _pad_lanes(w_kr), _pad_lanes(w_kr[:, perm])],
                                    axis=1).astype(BF16),
        "w_gates": wl[:, o_g:].astype(BF16),
        "g_na_qk": jnp.concatenate([jnp.tile(na_q_norm[l] * (NA_HEAD_DIM ** -0.5), NA_HEADS),
                                    jnp.tile(na_k_norm[l], NA_HEADS)]).reshape(1, 2 * naw),
        "g_cq": mla_cq_norm[l].reshape(1, q_rank),
        "g_ckv": mla_ckv_norm[l].reshape(1, kv_rank),
    }
    wq = mla_w_q_up[l].reshape(q_rank, MLA_HEADS, MLA_QK_DIM)
    wq_rope = wq[:, :, MLA_NOPE_DIM:]
    lw["w_q_up"] = jnp.concatenate([wq[:, :, :MLA_NOPE_DIM], _pad_lanes(wq_rope), _pad_lanes(wq_rope[:, :, perm])],
                                   axis=2).reshape(q_rank, MLA_HEADS * 3 * LANES).astype(BF16)
    lw["w_kv_up"] = mla_w_kv_up[l].astype(BF16)
    for tag, g in (("q", mla_q_norm[l]), ("k", mla_k_norm[l])):
        g_rope = g[MLA_NOPE_DIM:]
        lw[f"g{tag}_nope"] = g[:MLA_NOPE_DIM].reshape(1, LANES)
        lw[f"g{tag}_rope"] = _pad_lanes(g_rope).reshape(1, LANES)
        lw[f"g{tag}_swap"] = _pad_lanes(g_rope[perm]).reshape(1, LANES)
    lw["w_b_na"] = w_branch_na[l].astype(BF16)
    lw["w_b_mla"] = w_branch_mla[l].astype(BF16)
    lw["w_out"] = w_out[l].astype(BF16)
    lw["w_up"] = ffn_w_up[l].astype(BF16)
    lw["conv_w"] = ffn_conv_w[l]
    lw["conv_b"] = ffn_conv_b[l].reshape(1, -1)
    lw["w_down"] = ffn_w_down[l].astype(BF16)
    return lw


def _rope_tables(n_tok):
    t = jnp.arange(n_tok, dtype=jnp.int32)
    half = MLA_ROPE_DIM // 2
    inv_freq = ROPE_THETA ** (-jnp.arange(0, half, 2, dtype=F32) / half)
    cs, sn = [], []
    for pos in (t // GRID_W, t % GRID_W):
        ang = pos.astype(F32)[:, None] * inv_freq[None, :]
        cs += [jnp.cos(ang), jnp.cos(ang)]
        sn += [-jnp.sin(ang), jnp.sin(ang)]
    return _pad_lanes(jnp.concatenate(cs, axis=1)), _pad_lanes(jnp.concatenate(sn, axis=1))


def _identity_tables(n_tok):
    cos_t = _pad_lanes(jnp.ones((n_tok, MLA_ROPE_DIM), F32))
    return cos_t, jnp.zeros((n_tok, LANES), F32)


def _head_major(a, heads):
    rows = a.shape[0]
    return a.reshape(rows, heads, -1).transpose(1, 0, 2)


def kernel(x, c, ctx, c_ctx, ada_down, ada_up, ada_bias, norm_mix, norm_ffn, w_in, na_q_norm, na_k_norm, na_rpb, mla_cq_norm, mla_ckv_norm, mla_w_q_up, mla_w_kv_up, mla_q_norm, mla_k_norm, w_branch_na, w_branch_mla, w_out, ffn_w_up, ffn_conv_w, ffn_conv_b, ffn_w_down):
    batch, n_tok, d = x.shape
    assert batch == 1 and c.shape[0] == 1
    depth = w_in.shape[0]
    naw = NA_HEADS * NA_HEAD_DIM
    xs = x[0]
    cs = ctx[0]
    lat_tabs = _rope_tables(n_tok)
    ctx_tabs = _identity_tables(cs.shape[0])
    cond = jnp.zeros((8, d), F32).at[0].set(c[0]).at[1].set(c_ctx)

    for l in range(depth):
        last = l == depth - 1
        lw = _layer_weights(l, w_in, na_q_norm, na_k_norm, mla_cq_norm, mla_ckv_norm, mla_w_q_up,
                            mla_w_kv_up, mla_q_norm, mla_k_norm, w_branch_na, w_branch_mla, w_out,
                            ffn_w_up, ffn_conv_w, ffn_conv_b, ffn_w_down)
        mods = _adaln(cond, ada_down[l], ada_up[l], ada_bias[l])
        sh1, sc1, g1, sh2, sc2, g2 = [mods[0, i * d:(i + 1) * d] for i in range(N_MOD)]
        csh1, csc1, cg1, csh2, csc2, cg2 = [mods[1, i * d:(i + 1) * d] for i in range(N_MOD)]

        h = _norm_mod(xs, norm_mix[l], sc1, sh1)
        hc = _norm_mod(cs, norm_mix[l], csc1, csh1)
        na_qk, na_v, mq, mk, mv = _project(h, lw, lat_tabs)
        c_qk, c_v, cmq, cmk, cmv = _project(hc, lw, ctx_tabs)

        y_na = _na_attention(na_qk, na_v, c_qk, c_v, _na_bias_table(na_rpb[l]))
        y_mla = _flash(mq, mk, mv, (cmk, cmv), heads=MLA_HEADS, head_major=True, name="mla_attention")
        u = _merge(h, y_na, y_mla, lw)
        xs = _residual_matmul(u, lw["w_out"], xs, g1, "out_proj")
        h2 = _norm_mod(xs, norm_ffn[l], sc2, sh2)
        xs = _residual_matmul(_ffn_up(h2, lw), lw["w_down"], xs, g2, "ffn_down")

        if not last:
            yc_na = _flash(c_qk[:, :naw], c_qk[:, naw:], c_v, None, heads=NA_HEADS, head_major=False,
                           name="ctx_na_attention")
            yc_mla = _flash(cmq, cmk, cmv, None, heads=MLA_HEADS, head_major=True, name="ctx_mla_attention")
            uc = _merge(hc, yc_na, yc_mla, lw)
            cs = _residual_matmul(uc, lw["w_out"], cs, cg1, "ctx_out_proj")
            hc2 = _norm_mod(cs, norm_ffn[l], csc2, csh2)
            cs = _residual_matmul(_ffn_up(hc2, lw), lw["w_down"], cs, cg2, "ctx_ffn_down")
    return xs[None]
```

```python
import functools

import numpy as np
import jax
import jax.numpy as jnp
from jax import lax
from jax.experimental import pallas as pl
from jax.experimental.pallas import tpu as pltpu

F32 = jnp.float32
BF16 = jnp.bfloat16

GRID_W = 64
EPS = 1e-6
NEG_INF = -1e30
N_MOD = 6

NA_HEADS = 16
NA_HEAD_DIM = 128
NA_WIN_H = 8
NA_WIN_W = 16

MLA_HEADS = 16
MLA_NOPE_DIM = 128
MLA_ROPE_DIM = 64
MLA_QK_DIM = MLA_NOPE_DIM + MLA_ROPE_DIM
MLA_V_DIM = 128
ROPE_THETA = 10000.0
CONV_W = 3

LANES = 128
MLA_QK_PAD = 2 * LANES
BF16_SUBLANES = 16
VT_ROWS = MLA_V_DIM + BF16_SUBLANES
LOG2E = 1.4426950408889634
VMEM_LIMIT_BYTES = 56 * 1024 * 1024


def _tile(dim, pref, mult=8):
    if dim <= pref:
        return dim
    t = (pref // mult) * mult
    while t >= mult:
        if dim % t == 0:
            return t
        t -= mult
    return dim


def _params(*sem):
    return pltpu.CompilerParams(dimension_semantics=sem, vmem_limit_bytes=VMEM_LIMIT_BYTES)


def _ada_down_kernel(c_ref, w_ref, o_ref):
    c = c_ref[...]
    a = (c * jax.nn.sigmoid(c)).astype(BF16)
    o_ref[...] = jnp.dot(a, w_ref[...].astype(BF16), preferred_element_type=F32)


def _ada_up_kernel(t_ref, w_ref, b_ref, o_ref):
    o_ref[...] = jnp.dot(t_ref[...].astype(BF16), w_ref[...].astype(BF16),
                         preferred_element_type=F32) + b_ref[...]


def _adaln(cond, down, up, bias):
    rows, d = cond.shape
    rank = down.shape[1]
    n = up.shape[1]
    bn1 = _tile(rank, 512, LANES)
    t = pl.pallas_call(
        _ada_down_kernel,
        grid=(rank // bn1,),
        in_specs=[pl.BlockSpec((rows, d), lambda j: (0, 0)),
                  pl.BlockSpec((d, bn1), lambda j: (0, j))],
        out_specs=pl.BlockSpec((rows, bn1), lambda j: (0, j)),
        out_shape=jax.ShapeDtypeStruct((rows, rank), F32),
        compiler_params=_params("parallel"),
        name="ada_down",
    )(cond, down)
    bn2 = _tile(n, 2048, LANES)
    return pl.pallas_call(
        _ada_up_kernel,
        grid=(n // bn2,),
        in_specs=[pl.BlockSpec((rows, rank), lambda j: (0, 0)),
                  pl.BlockSpec((rank, bn2), lambda j: (0, j)),
                  pl.BlockSpec((1, bn2), lambda j: (0, j))],
        out_specs=pl.BlockSpec((rows, bn2), lambda j: (0, j)),
        out_shape=jax.ShapeDtypeStruct((rows, n), F32),
        compiler_params=_params("parallel"),
        name="ada_up",
    )(t, up, bias.reshape(1, n))


def _norm_mod_kernel(x_ref, g_ref, sc_ref, sh_ref, o_ref):
    x = x_ref[...]
    y = x * lax.rsqrt(jnp.mean(x * x, axis=-1, keepdims=True) + EPS) * g_ref[...]
    o_ref[...] = (y * (1.0 + sc_ref[...]) + sh_ref[...]).astype(o_ref.dtype)


def _norm_mod(x, g, sc, sh):
    m, d = x.shape
    bm = _tile(m, 256)
    vec = pl.BlockSpec((1, d), lambda i: (0, 0))
    return pl.pallas_call(
        _norm_mod_kernel,
        grid=(m // bm,),
        in_specs=[pl.BlockSpec((bm, d), lambda i: (i, 0)), vec, vec, vec],
        out_specs=pl.BlockSpec((bm, d), lambda i: (i, 0)),
        out_shape=jax.ShapeDtypeStruct((m, d), BF16),
        compiler_params=_params("parallel"),
        name="norm_mod",
    )(x, g.reshape(1, d), sc.reshape(1, d), sh.reshape(1, d))


def _mm_kernel(x_ref, w_ref, *rest, epilogue):
    acc = jnp.dot(x_ref[...], w_ref[...], preferred_element_type=F32)
    epilogue(acc, *rest)


def _matmul(x, w, epilogue, extras, extra_specs, out_shapes, out_specs, bm, bn, name):
    m, k = x.shape
    n = w.shape[1]
    return pl.pallas_call(
        functools.partial(_mm_kernel, epilogue=epilogue),
        grid=(m // bm, n // bn),
        in_specs=[pl.BlockSpec((bm, k), lambda i, j: (i, 0)),
                  pl.BlockSpec((k, bn), lambda i, j: (0, j))] + list(extra_specs),
        out_specs=out_specs,
        out_shape=out_shapes,
        compiler_params=_params("parallel", "parallel"),
        name=name,
    )(x, w, *extras)


def _epi_cast(acc, o_ref):
    o_ref[...] = acc.astype(o_ref.dtype)


def _epi_headnorm(acc, g_ref, o_ref):
    for c in range(acc.shape[1] // LANES):
        sl = slice(c * LANES, (c + 1) * LANES)
        a = acc[:, sl]
        r = lax.rsqrt(jnp.mean(a * a, axis=-1, keepdims=True) + EPS)
        o_ref[:, sl] = (a * r * g_ref[:, sl]).astype(o_ref.dtype)


def _epi_rownorm(acc, g_ref, o_ref):
    r = lax.rsqrt(jnp.mean(acc * acc, axis=-1, keepdims=True) + EPS)
    o_ref[...] = (acc * r * g_ref[...]).astype(o_ref.dtype)


def _epi_ckv(acc, g_ref, ckv_ref, kr_ref, *, kv_rank):
    a = acc[:, :kv_rank]
    r = lax.rsqrt(jnp.mean(a * a, axis=-1, keepdims=True) + EPS)
    ckv_ref[...] = (a * r * g_ref[...]).astype(ckv_ref.dtype)
    kr_ref[...] = acc[:, kv_rank:]


def _epi_q_up(acc, c_ref, s_ref, gn_ref, gr_ref, gs_ref, o_ref, *, heads, scale):
    cg = c_ref[...] * gr_ref[...]
    sg = s_ref[...] * gs_ref[...]
    for h in range(heads):
        base = h * 3 * LANES
        nope = acc[:, base:base + LANES]
        rp = acc[:, base + LANES:base + 2 * LANES]
        sw = acc[:, base + 2 * LANES:base + 3 * LANES]
        ss = jnp.sum(nope * nope, axis=-1, keepdims=True) + jnp.sum(rp * rp, axis=-1, keepdims=True)
        r = lax.rsqrt(ss * (1.0 / MLA_QK_DIM) + EPS) * scale
        o_ref[h, :, 0:LANES] = (nope * r * gn_ref[...]).astype(o_ref.dtype)
        o_ref[h, :, LANES:2 * LANES] = ((rp * cg + sw * sg) * r).astype(o_ref.dtype)


def _ones_row_tile(cols, dtype):
    rows = lax.broadcasted_iota(jnp.int32, (BF16_SUBLANES, cols), 0)
    return jnp.where(rows == 0, 1.0, 0.0).astype(dtype)


def _epi_kv_up(acc, kr_ref, c_ref, s_ref, gn_ref, gr_ref, gs_ref, k_ref, vt_ref, *, heads):
    kr = kr_ref[:, 0:LANES]
    krs = kr_ref[:, LANES:2 * LANES]
    ssr = jnp.sum(kr * kr, axis=-1, keepdims=True)
    rope = kr * (c_ref[...] * gr_ref[...]) + krs * (s_ref[...] * gs_ref[...])
    for h in range(heads):
        base = h * 2 * LANES
        nope = acc[:, base:base + LANES]
        ss = jnp.sum(nope * nope, axis=-1, keepdims=True) + ssr
        r = lax.rsqrt(ss * (1.0 / MLA_QK_DIM) + EPS)
        k_ref[h, :, 0:LANES] = (nope * r * gn_ref[...]).astype(k_ref.dtype)
        k_ref[h, :, LANES:2 * LANES] = (rope * r).astype(k_ref.dtype)
        vt_ref[h, 0:MLA_V_DIM, :] = acc[:, base + LANES:base + 2 * LANES].T.astype(vt_ref.dtype)
        vt_ref[h, MLA_V_DIM:VT_ROWS, :] = _ones_row_tile(acc.shape[0], vt_ref.dtype)


def _epi_residual(acc, x_ref, g_ref, o_ref):
    o_ref[...] = x_ref[...] + g_ref[...] * acc


def _project(h, lw, tabs):
    m, d = h.shape
    bm = _tile(m, 1024)
    naw = NA_HEADS * NA_HEAD_DIM
    row = lambda n: pl.BlockSpec((1, n), lambda i, j: (0, 0))

    bn = _tile(2 * naw, 1024, LANES)
    na_qk = _matmul(h, lw["w_na_qk"], _epi_headnorm, [lw["g_na_qk"]],
                    [pl.BlockSpec((1, bn), lambda i, j: (0, j))],
                    jax.ShapeDtypeStruct((m, 2 * naw), BF16),
                    pl.BlockSpec((bm, bn), lambda i, j: (i, j)), bm, bn, "na_qk_proj")
    bn = _tile(naw, 1024, LANES)
    na_v = _matmul(h, lw["w_na_v"], _epi_cast, [], [],
                   jax.ShapeDtypeStruct((m, naw), BF16),
                   pl.BlockSpec((bm, bn), lambda i, j: (i, j)), bm, bn, "na_v_proj")

    q_rank = lw["w_cq"].shape[1]
    kv_rank = lw["w_ckv_kr"].shape[1] - 2 * LANES
    bmc = _tile(m, 512)
    cqn = _matmul(h, lw["w_cq"], _epi_rownorm, [lw["g_cq"]], [row(q_rank)],
                  jax.ShapeDtypeStruct((m, q_rank), BF16),
                  pl.BlockSpec((bmc, q_rank), lambda i, j: (i, 0)), bmc, q_rank, "cq_proj")
    ckvn, kr = _matmul(h, lw["w_ckv_kr"], functools.partial(_epi_ckv, kv_rank=kv_rank),
                       [lw["g_ckv"]], [row(kv_rank)],
                       (jax.ShapeDtypeStruct((m, kv_rank), BF16),
                        jax.ShapeDtypeStruct((m, 2 * LANES), F32)),
                       (pl.BlockSpec((bmc, kv_rank), lambda i, j: (i, 0)),
                        pl.BlockSpec((bmc, 2 * LANES), lambda i, j: (i, 0))),
                       bmc, kv_rank + 2 * LANES, "ckv_proj")

    cos_t, sin_t = tabs
    hb = 4 if MLA_HEADS % 4 == 0 else 1
    bmq = _tile(m, 512)
    tab = pl.BlockSpec((bmq, LANES), lambda i, j: (i, 0))
    vec = pl.BlockSpec((1, LANES), lambda i, j: (0, 0))
    mla_q = _matmul(cqn, lw["w_q_up"],
                    functools.partial(_epi_q_up, heads=hb, scale=MLA_QK_DIM ** -0.5 * LOG2E),
                    [cos_t, sin_t, lw["gq_nope"], lw["gq_rope"], lw["gq_swap"]],
                    [tab, tab, vec, vec, vec],
                    jax.ShapeDtypeStruct((MLA_HEADS, m, MLA_QK_PAD), BF16),
                    pl.BlockSpec((hb, bmq, MLA_QK_PAD), lambda i, j: (j, i, 0)),
                    bmq, hb * 3 * LANES, "mla_q_up")
    mla_k, mla_v = _matmul(ckvn, lw["w_kv_up"], functools.partial(_epi_kv_up, heads=hb),
                           [kr, cos_t, sin_t, lw["gk_nope"], lw["gk_rope"], lw["gk_swap"]],
                           [pl.BlockSpec((bmq, 2 * LANES), lambda i, j: (i, 0)), tab, tab, vec, vec, vec],
                           (jax.ShapeDtypeStruct((MLA_HEADS, m, MLA_QK_PAD), BF16),
                            jax.ShapeDtypeStruct((MLA_HEADS, VT_ROWS, m), BF16)),
                           (pl.BlockSpec((hb, bmq, MLA_QK_PAD), lambda i, j: (j, i, 0)),
                            pl.BlockSpec((hb, VT_ROWS, bmq), lambda i, j: (j, 0, i))),
                           bmq, hb * 2 * LANES, "mla_kv_up")
    return na_qk, na_v, mla_q, mla_k, mla_v


def _dot_nt(a, b):
    return lax.dot_general(a, b, (((1,), (1,)), ((), ())), preferred_element_type=F32)


def _softmax_pv(s_t, vt_blk, state):
    blk_max = jnp.max(s_t, axis=0, keepdims=True)
    m_new = blk_max if state is None else jnp.maximum(state[0], blk_max)
    p_t = jnp.exp2(s_t - m_new).astype(BF16)
    upd = jnp.dot(vt_blk, p_t, preferred_element_type=F32)
    if state is None:
        return m_new, upd
    return m_new, jnp.exp2(state[0] - m_new) * state[1] + upd


def _flash_kernel(*refs, tk, n_chunks, has_ctx):
    if has_ctx:
        q_ref, k_ref, vt_ref, kc_ref, vct_ref, o_ref, s_scr = refs
    else:
        q_ref, k_ref, vt_ref, o_ref, s_scr = refs
    q = q_ref[...]

    def chunk(c):
        return pl.ds(pl.multiple_of(c * tk, tk), tk)

    def scores(c, slot):
        s_scr[slot] = _dot_nt(k_ref[chunk(c), :], q)

    def consume(c, slot, state):
        return _softmax_pv(s_scr[slot], vt_ref[:, chunk(c)], state)

    if has_ctx:
        state = _softmax_pv(_dot_nt(kc_ref[...], q), vct_ref[...], None)
        first = 0
    else:
        state = _softmax_pv(_dot_nt(k_ref[0:tk, :], q), vt_ref[:, 0:tk], None)
        first = 1

    n_pairs = (n_chunks - first) // 2
    if n_pairs > 0:
        scores(first, 0)

        def pair(j, state, prefetch):
            c0 = first + 2 * j
            scores(c0 + 1, 1)
            state = consume(c0, 0, state)
            if prefetch:
                scores(c0 + 2, 0)
            return consume(c0 + 1, 1, state)

        state = lax.fori_loop(0, n_pairs - 1, lambda j, st: pair(j, st, True), state)
        state = pair(n_pairs - 1, state, False)
    if (n_chunks - first) % 2:
        c = n_chunks - 1
        state = _softmax_pv(_dot_nt(k_ref[c * tk:(c + 1) * tk, :], q), vt_ref[:, c * tk:(c + 1) * tk], state)

    acc = state[1]
    o_t = acc[0:MLA_V_DIM, :] / acc[MLA_V_DIM:MLA_V_DIM + 1, :]
    o_ref[...] = o_t.T.astype(o_ref.dtype)


def _flash(q, k, vt, ctx_kv, *, name):
    heads, mq, dq = q.shape
    mk = k.shape[1]
    tq = _tile(mq, 512)
    tk = _tile(mk, 2048, LANES)
    q_spec = pl.BlockSpec((None, tq, dq), lambda h, i: (h, i, 0))
    whole = lambda a: pl.BlockSpec((None,) + a.shape[1:], lambda h, i: (h, 0, 0))
    in_specs = [q_spec, whole(k), whole(vt)]
    args = [q, k, vt]
    if ctx_kv is not None:
        in_specs += [whole(ctx_kv[0]), whole(ctx_kv[1])]
        args += list(ctx_kv)
    return pl.pallas_call(
        functools.partial(_flash_kernel, tk=tk, n_chunks=mk // tk, has_ctx=ctx_kv is not None),
        grid=(heads, mq // tq),
        in_specs=in_specs,
        out_specs=pl.BlockSpec((tq, MLA_V_DIM), lambda h, i: (i, h)),
        out_shape=jax.ShapeDtypeStruct((mq, heads * MLA_V_DIM), BF16),
        scratch_shapes=[pltpu.VMEM((2, tk, tq), F32)],
        compiler_params=_params("parallel", "parallel"),
        name=name,
    )(*args)


def _na_kernel(q_ref, k_ref, v_ref, kc_ref, vc_ref, b_ref, o_ref):
    head = lambda h: slice(h * NA_HEAD_DIM, (h + 1) * NA_HEAD_DIM)
    scores = [_dot_nt(q_ref[:, head(h)], jnp.concatenate([k_ref[:, head(h)], kc_ref[:, head(h)]], axis=0))
              + b_ref[0, h] for h in range(NA_HEADS)]
    probs, inv_l = [], []
    for s in scores:
        p = jnp.exp2(s - jnp.max(s, axis=-1, keepdims=True))
        inv_l.append(1.0 / jnp.sum(p, axis=-1, keepdims=True))
        probs.append(p.astype(BF16))
    outs = [jnp.dot(probs[h], jnp.concatenate([v_ref[:, head(h)], vc_ref[:, head(h)]], axis=0),
                    preferred_element_type=F32) * inv_l[h] for h in range(NA_HEADS)]
    o_ref[...] = jnp.concatenate(outs, axis=-1).astype(o_ref.dtype)


def _na_bias_table(rpb, n_ctx):
    qc = np.arange(GRID_W)
    kc = np.arange(GRID_W)
    cs = np.clip(qc - NA_WIN_W // 2, 0, GRID_W - NA_WIN_W)
    col_ok = (kc[None, :] >= cs[:, None]) & (kc[None, :] < cs[:, None] + NA_WIN_W)
    col_idx = np.clip(kc[None, :] - qc[:, None] + NA_WIN_W - 1, 0, 2 * NA_WIN_W - 2)
    cols = rpb.astype(F32)[:, :, col_idx]
    cols = jnp.where(jnp.asarray(col_ok)[None, None], cols * LOG2E, NEG_INF)
    variants = [cols[:, v:v + NA_WIN_H].transpose(0, 2, 1, 3).reshape(NA_HEADS, GRID_W, NA_WIN_H * GRID_W)
                for v in range(NA_WIN_H)]
    table = jnp.stack(variants)
    return jnp.concatenate([table, jnp.zeros(table.shape[:3] + (n_ctx,), F32)], axis=-1)


def _na_attention(na_qk, na_v, ctx_qk, ctx_v, bias):
    m = na_qk.shape[0]
    naw = NA_HEADS * NA_HEAD_DIM
    rows_n = m // GRID_W
    win = NA_WIN_H * GRID_W
    n_ctx = ctx_qk.shape[0]

    def start(r):
        return jnp.clip(r - NA_WIN_H // 2, 0, rows_n - NA_WIN_H)

    E = pl.Element
    return pl.pallas_call(
        _na_kernel,
        grid=(rows_n,),
        in_specs=[
            pl.BlockSpec((GRID_W, naw), lambda r: (r, 0)),
            pl.BlockSpec((E(win), E(naw)), lambda r: (start(r) * GRID_W, naw)),
            pl.BlockSpec((E(win), E(naw)), lambda r: (start(r) * GRID_W, 0)),
            pl.BlockSpec((n_ctx, naw), lambda r: (0, 1)),
            pl.BlockSpec((n_ctx, naw), lambda r: (0, 0)),
            pl.BlockSpec((1, NA_HEADS, GRID_W, win + n_ctx), lambda r: (start(r) - r + NA_WIN_H - 1, 0, 0, 0)),
        ],
        out_specs=pl.BlockSpec((GRID_W, naw), lambda r: (r, 0)),
        out_shape=jax.ShapeDtypeStruct((m, naw), BF16),
        compiler_params=_params("parallel"),
        name="na_attention",
    )(na_qk, na_qk, na_v, ctx_qk, ctx_v, bias)


def _merge_kernel(h_ref, yn_ref, ym_ref, wgn_ref, wgm_ref, wbn_ref, wbm_ref, o_ref):
    h = h_ref[...]
    gn = jax.nn.sigmoid(jnp.dot(h, wgn_ref[...], preferred_element_type=F32))
    gm = jax.nn.sigmoid(jnp.dot(h, wgm_ref[...], preferred_element_type=F32))
    a = jnp.dot(yn_ref[...], wbn_ref[...], preferred_element_type=F32)
    b = jnp.dot(ym_ref[...], wbm_ref[...], preferred_element_type=F32)
    o_ref[...] = (gn * a + gm * b).astype(o_ref.dtype)


def _merge(h, y_na, y_mla, lw):
    m, d = h.shape
    bm = _tile(m, 512)
    bn = _tile(d, 512, LANES)
    nj = d // bn
    kn, km = y_na.shape[1], y_mla.shape[1]
    return pl.pallas_call(
        _merge_kernel,
        grid=(m // bm, nj),
        in_specs=[pl.BlockSpec((bm, d), lambda i, j: (i, 0)),
                  pl.BlockSpec((bm, kn), lambda i, j: (i, 0)),
                  pl.BlockSpec((bm, km), lambda i, j: (i, 0)),
                  pl.BlockSpec((d, bn), lambda i, j: (0, j)),
                  pl.BlockSpec((d, bn), lambda i, j: (0, j + nj)),
                  pl.BlockSpec((kn, bn), lambda i, j: (0, j)),
                  pl.BlockSpec((km, bn), lambda i, j: (0, j))],
        out_specs=pl.BlockSpec((bm, bn), lambda i, j: (i, j)),
        out_shape=jax.ShapeDtypeStruct((m, d), BF16),
        compiler_params=_params("parallel", "parallel"),
        name="branch_merge",
    )(h, y_na, y_mla, lw["w_gates"], lw["w_gates"], lw["w_b_na"], lw["w_b_mla"])


def _residual_matmul(a, w, x, gate, name):
    m, d = x.shape
    bm = _tile(m, 512)
    bn = _tile(d, 1024, LANES)
    return _matmul(a, w, _epi_residual, [x, gate.reshape(1, d)],
                   [pl.BlockSpec((bm, bn), lambda i, j: (i, j)),
                    pl.BlockSpec((1, bn), lambda i, j: (0, j))],
                   jax.ShapeDtypeStruct((m, d), F32),
                   pl.BlockSpec((bm, bn), lambda i, j: (i, j)), bm, bn, name)


def _ffn_up_kernel(x_ref, wg_ref, wv_ref, halo_ref, cw_ref, cb_ref, o_ref):
    x = x_ref[...]
    g = jnp.dot(x, wg_ref[...], preferred_element_type=F32)
    bm = g.shape[0]
    rows = lax.broadcasted_iota(jnp.int32, g.shape, 0)
    g_prev = jnp.where(rows == 0, halo_ref[0:1, :], pltpu.roll(g, 1, axis=0))
    g_next = jnp.where(rows == bm - 1, halo_ref[1:2, :], pltpu.roll(g, bm - 1, axis=0))
    y = g_prev * cw_ref[0:1, :] + g * cw_ref[1:2, :] + g_next * cw_ref[2:3, :] + cb_ref[...]
    v = jnp.dot(x, wv_ref[...], preferred_element_type=F32)
    o_ref[...] = (y * jax.nn.sigmoid(y) * v).astype(o_ref.dtype)


def _ffn_up(h2, lw):
    m, d = h2.shape
    dff = lw["conv_b"].shape[1]
    bm = _tile(m, 1024)
    bn = _tile(dff, 512, LANES)
    nt, nj = m // bm, dff // bn

    edge = h2.reshape(nt, bm, d)[:, (0, bm - 1), :].reshape(2 * nt, d)
    pad = (-edge.shape[0]) % 16
    edge = jnp.pad(edge, ((0, pad), (0, 0)))
    bne = _tile(dff, 1024, LANES)
    eg = _matmul(edge, lw["w_up"][:, :dff], _epi_cast, [], [],
                 jax.ShapeDtypeStruct((edge.shape[0], dff), F32),
                 pl.BlockSpec((edge.shape[0], bne), lambda i, j: (i, j)),
                 edge.shape[0], bne, "ffn_edge_rows")
    eg = eg[:2 * nt].reshape(nt, 2, dff)
    zero = jnp.zeros((1, dff), F32)
    prev_rows = jnp.concatenate([zero, eg[:-1, 1]], axis=0)
    next_rows = jnp.concatenate([eg[1:, 0], zero], axis=0)
    halo = jnp.stack([prev_rows, next_rows], axis=1)

    return pl.pallas_call(
        _ffn_up_kernel,
        grid=(m // bm, nj),
        in_specs=[pl.BlockSpec((bm, d), lambda i, j: (i, 0)),
                  pl.BlockSpec((d, bn), lambda i, j: (0, j)),
                  pl.BlockSpec((d, bn), lambda i, j: (0, j + nj)),
                  pl.BlockSpec((None, 2, bn), lambda i, j: (i, 0, j)),
                  pl.BlockSpec((CONV_W, bn), lambda i, j: (0, j)),
                  pl.BlockSpec((1, bn), lambda i, j: (0, j))],
        out_specs=pl.BlockSpec((bm, bn), lambda i, j: (i, j)),
        out_shape=jax.ShapeDtypeStruct((m, dff), BF16),
        compiler_params=_params("parallel", "parallel"),
        name="ffn_up_conv",
    )(h2, lw["w_up"], lw["w_up"], halo, lw["conv_w"], lw["conv_b"])


def _rope_partner():
    half = MLA_ROPE_DIM // 2
    quarter = half // 2
    idx = np.arange(MLA_ROPE_DIM)
    return np.where((idx % half) < quarter, idx + quarter, idx - quarter)


def _pad_lanes(a):
    return jnp.pad(a, [(0, 0)] * (a.ndim - 1) + [(0, LANES - a.shape[-1])])


def _layer_weights(l, w_in, na_q_norm, na_k_norm, mla_cq_norm, mla_ckv_norm, mla_w_q_up, mla_w_kv_up,
                   mla_q_norm, mla_k_norm, w_branch_na, w_branch_mla, w_out,
                   ffn_w_up, ffn_conv_w, ffn_conv_b, ffn_w_down):
    naw = NA_HEADS * NA_HEAD_DIM
    q_rank = mla_w_q_up.shape[1]
    kv_rank = mla_w_kv_up.shape[1]
    perm = _rope_partner()
    wl = w_in[l]
    o_cq = 3 * naw
    o_ckv = o_cq + q_rank
    o_kr = o_ckv + kv_rank
    o_g = o_kr + MLA_ROPE_DIM
    w_kr = wl[:, o_kr:o_g]
    lw = {
        "w_na_qk": wl[:, :2 * naw].astype(BF16),
        "w_na_v": wl[:, 2 * naw:3 * naw].astype(BF16),
        "w_cq": wl[:, o_cq:o_ckv].astype(BF16),
        "w_ckv_kr": jnp.concatenate([wl[:, o_ckv:o_kr], _pad_lanes(w_kr), _pad_lanes(w_kr[:, perm])],
                                    axis=1).astype(BF16),
        "w_gates": wl[:, o_g:].astype(BF16),
        "g_na_qk": jnp.concatenate([jnp.tile(na_q_norm[l] * (NA_HEAD_DIM ** -0.5 * LOG2E), NA_HEADS),
                                    jnp.tile(na_k_norm[l], NA_HEADS)]).reshape(1, 2 * naw),
        "g_cq": mla_cq_norm[l].reshape(1, q_rank),
        "g_ckv": mla_ckv_norm[l].reshape(1, kv_rank),
    }
    wq = mla_w_q_up[l].reshape(q_rank, MLA_HEADS, MLA_QK_DIM)
    wq_rope = wq[:, :, MLA_NOPE_DIM:]
    lw["w_q_up"] = jnp.concatenate([wq[:, :, :MLA_NOPE_DIM], _pad_lanes(wq_rope), _pad_lanes(wq_rope[:, :, perm])],
                                   axis=2).reshape(q_rank, MLA_HEADS * 3 * LANES).astype(BF16)
    lw["w_kv_up"] = mla_w_kv_up[l].astype(BF16)
    for tag, g in (("q", mla_q_norm[l]), ("k", mla_k_norm[l])):
        g_rope = g[MLA_NOPE_DIM:]
        lw[f"g{tag}_nope"] = g[:MLA_NOPE_DIM].reshape(1, LANES)
        lw[f"g{tag}_rope"] = _pad_lanes(g_rope).reshape(1, LANES)
        lw[f"g{tag}_swap"] = _pad_lanes(g_rope[perm]).reshape(1, LANES)
    lw["w_b_na"] = w_branch_na[l].astype(BF16)
    lw["w_b_mla"] = w_branch_mla[l].astype(BF16)
    lw["w_out"] = w_out[l].astype(BF16)
    lw["w_up"] = ffn_w_up[l].astype(BF16)
    lw["conv_w"] = ffn_conv_w[l]
    lw["conv_b"] = ffn_conv_b[l].reshape(1, -1)
    lw["w_down"] = ffn_w_down[l].astype(BF16)
    return lw


def _rope_tables(n_tok):
    t = jnp.arange(n_tok, dtype=jnp.int32)
    half = MLA_ROPE_DIM // 2
    inv_freq = ROPE_THETA ** (-jnp.arange(0, half, 2, dtype=F32) / half)
    cs, sn = [], []
    for pos in (t // GRID_W, t % GRID_W):
        ang = pos.astype(F32)[:, None] * inv_freq[None, :]
        cs += [jnp.cos(ang), jnp.cos(ang)]
        sn += [-jnp.sin(ang), jnp.sin(ang)]
    return _pad_lanes(jnp.concatenate(cs, axis=1)), _pad_lanes(jnp.concatenate(sn, axis=1))


def _identity_tables(n_tok):
    cos_t = _pad_lanes(jnp.ones((n_tok, MLA_ROPE_DIM), F32))
    return cos_t, jnp.zeros((n_tok, LANES), F32)


def _head_major(a, heads):
    rows = a.shape[0]
    return a.reshape(rows, heads, -1).transpose(1, 0, 2)


def _head_major_vt(v, heads):
    rows = v.shape[0]
    vt = v.reshape(rows, heads, MLA_V_DIM).transpose(1, 2, 0)
    tail = jnp.zeros((heads, BF16_SUBLANES, rows), v.dtype).at[:, 0, :].set(1.0)
    return jnp.concatenate([vt, tail], axis=1)


def kernel(x, c, ctx, c_ctx, ada_down, ada_up, ada_bias, norm_mix, norm_ffn, w_in, na_q_norm, na_k_norm, na_rpb, mla_cq_norm, mla_ckv_norm, mla_w_q_up, mla_w_kv_up, mla_q_norm, mla_k_norm, w_branch_na, w_branch_mla, w_out, ffn_w_up, ffn_conv_w, ffn_conv_b, ffn_w_down):
    batch, n_tok, d = x.shape
    assert batch == 1 and c.shape[0] == 1
    depth = w_in.shape[0]
    naw = NA_HEADS * NA_HEAD_DIM
    xs = x[0]
    cs = ctx[0]
    lat_tabs = _rope_tables(n_tok)
    ctx_tabs = _identity_tables(cs.shape[0])
    cond = jnp.zeros((8, d), F32).at[0].set(c[0]).at[1].set(c_ctx)

    for l in range(depth):
        last = l == depth - 1
        lw = _layer_weights(l, w_in, na_q_norm, na_k_norm, mla_cq_norm, mla_ckv_norm, mla_w_q_up,
                            mla_w_kv_up, mla_q_norm, mla_k_norm, w_branch_na, w_branch_mla, w_out,
                            ffn_w_up, ffn_conv_w, ffn_conv_b, ffn_w_down)
        mods = _adaln(cond, ada_down[l], ada_up[l], ada_bias[l])
        sh1, sc1, g1, sh2, sc2, g2 = [mods[0, i * d:(i + 1) * d] for i in range(N_MOD)]
        csh1, csc1, cg1, csh2, csc2, cg2 = [mods[1, i * d:(i + 1) * d] for i in range(N_MOD)]

        h = _norm_mod(xs, norm_mix[l], sc1, sh1)
        hc = _norm_mod(cs, norm_mix[l], csc1, csh1)
        na_qk, na_v, mq, mk, mv = _project(h, lw, lat_tabs)
        c_qk, c_v, cmq, cmk, cmv = _project(hc, lw, ctx_tabs)

        y_na = _na_attention(na_qk, na_v, c_qk, c_v, _na_bias_table(na_rpb[l], cs.shape[0]))
        y_mla = _flash(mq, mk, mv, (cmk, cmv), name="mla_attention")
        u = _merge(h, y_na, y_mla, lw)
        xs = _residual_matmul(u, lw["w_out"], xs, g1, "out_proj")
        h2 = _norm_mod(xs, norm_ffn[l], sc2, sh2)
        xs = _residual_matmul(_ffn_up(h2, lw), lw["w_down"], xs, g2, "ffn_down")

        if not last:
            yc_na = _flash(_head_major(c_qk[:, :naw], NA_HEADS), _head_major(c_qk[:, naw:], NA_HEADS),
                           _head_major_vt(c_v, NA_HEADS), None, name="ctx_na_attention")
            yc_mla = _flash(cmq, cmk, cmv, None, name="ctx_mla_attention")
            uc = _merge(hc, yc_na, yc_mla, lw)
            cs = _residual_matmul(uc, lw["w_out"], cs, cg1, "ctx_out_proj")
            hc2 = _norm_mod(cs, norm_ffn[l], csc2, csh2)
            cs = _residual_matmul(_ffn_up(hc2, lw), lw["w_down"], cs, cg2, "ctx_ffn_down")
    return xs[None]
```

```python
import functools
from typing import NamedTuple

import numpy as np
import jax
import jax.numpy as jnp
from jax import lax
from jax.experimental import pallas as pl
from jax.experimental.pallas import tpu as pltpu

F32 = jnp.float32
BF16 = jnp.bfloat16

GRID_W = 64
EPS = 1e-6
NEG_INF = -1e30
N_MOD = 6

NA_HEADS = 16
NA_HEAD_DIM = 128
NA_WIN_H = 8
NA_WIN_W = 16

MLA_HEADS = 16
MLA_NOPE_DIM = 128
MLA_ROPE_DIM = 64
MLA_QK_DIM = MLA_NOPE_DIM + MLA_ROPE_DIM
MLA_V_DIM = 128
ROPE_THETA = 10000.0
CONV_W = 3

LANES = 128
MLA_QK_PAD = 2 * LANES
BF16_SUBLANES = 16
VT_ROWS = MLA_V_DIM + BF16_SUBLANES
LOG2E = 1.4426950408889634
VMEM_LIMIT_BYTES = 56 * 1024 * 1024


def _tile(dim, pref, mult=8):
    if dim <= pref:
        return dim
    t = (pref // mult) * mult
    while t >= mult:
        if dim % t == 0:
            return t
        t -= mult
    return dim


def _params(*sem):
    return pltpu.CompilerParams(dimension_semantics=sem, vmem_limit_bytes=VMEM_LIMIT_BYTES)


def _ada_down_kernel(c_ref, w_ref, o_ref):
    c = c_ref[...]
    a = (c * jax.nn.sigmoid(c)).astype(BF16)
    o_ref[...] = jnp.dot(a, w_ref[...].astype(BF16), preferred_element_type=F32)


def _ada_up_kernel(t_ref, w_ref, b_ref, o_ref):
    o_ref[...] = jnp.dot(t_ref[...].astype(BF16), w_ref[...].astype(BF16),
                         preferred_element_type=F32) + b_ref[...]


def _adaln(cond, down, up, bias, layer):
    rows, d = cond.shape
    rank = down.shape[2]
    n = up.shape[2]
    bn1 = _tile(rank, 512, LANES)
    t = pl.pallas_call(
        _ada_down_kernel,
        grid=(rank // bn1,),
        in_specs=[pl.BlockSpec((rows, d), lambda j: (0, 0)),
                  pl.BlockSpec((None, d, bn1), lambda j: (layer, 0, j))],
        out_specs=pl.BlockSpec((rows, bn1), lambda j: (0, j)),
        out_shape=jax.ShapeDtypeStruct((rows, rank), F32),
        compiler_params=_params("parallel"),
        name="ada_down",
    )(cond, down)
    bn2 = _tile(n, 2048, LANES)
    return pl.pallas_call(
        _ada_up_kernel,
        grid=(n // bn2,),
        in_specs=[pl.BlockSpec((rows, rank), lambda j: (0, 0)),
                  pl.BlockSpec((None, rank, bn2), lambda j: (layer, 0, j)),
                  pl.BlockSpec((None, 1, bn2), lambda j: (layer, 0, j))],
        out_specs=pl.BlockSpec((rows, bn2), lambda j: (0, j)),
        out_shape=jax.ShapeDtypeStruct((rows, n), F32),
        compiler_params=_params("parallel"),
        name="ada_up",
    )(t, up, bias.reshape(bias.shape[0], 1, n))


def _norm_mod_kernel(x_ref, g_ref, sc_ref, sh_ref, o_ref):
    x = x_ref[...]
    y = x * lax.rsqrt(jnp.mean(x * x, axis=-1, keepdims=True) + EPS) * g_ref[...]
    o_ref[...] = (y * (1.0 + sc_ref[...]) + sh_ref[...]).astype(o_ref.dtype)


def _norm_mod(x, g, sc, sh):
    m, d = x.shape
    bm = _tile(m, 256)
    vec = pl.BlockSpec((1, d), lambda i: (0, 0))
    return pl.pallas_call(
        _norm_mod_kernel,
        grid=(m // bm,),
        in_specs=[pl.BlockSpec((bm, d), lambda i: (i, 0)), vec, vec, vec],
        out_specs=pl.BlockSpec((bm, d), lambda i: (i, 0)),
        out_shape=jax.ShapeDtypeStruct((m, d), BF16),
        compiler_params=_params("parallel"),
        name="norm_mod",
    )(x, g.reshape(1, d), sc.reshape(1, d), sh.reshape(1, d))


def _mm_kernel(x_ref, w_ref, *rest, epilogue):
    acc = jnp.dot(x_ref[...], w_ref[...], preferred_element_type=F32)
    epilogue(acc, *rest)


class _W(NamedTuple):
    arr: jax.Array
    layer: int
    col0: int
    n: int


def _w_spec(w, bn, shift=0):
    blk0, rem = divmod(w.col0, bn)
    assert rem == 0 and w.n % bn == 0
    return pl.BlockSpec((None, w.arr.shape[1], bn), lambda i, j: (w.layer, 0, blk0 + shift + j))


def _matmul(x, w, epilogue, extras, extra_specs, out_shapes, out_specs, bm, bn, name):
    m, k = x.shape
    return pl.pallas_call(
        functools.partial(_mm_kernel, epilogue=epilogue),
        grid=(m // bm, w.n // bn),
        in_specs=[pl.BlockSpec((bm, k), lambda i, j: (i, 0)), _w_spec(w, bn)] + list(extra_specs),
        out_specs=out_specs,
        out_shape=out_shapes,
        compiler_params=_params("parallel", "parallel"),
        name=name,
    )(x, w.arr, *extras)


def _epi_cast(acc, o_ref):
    o_ref[...] = acc.astype(o_ref.dtype)


def _epi_headnorm(acc, g_ref, o_ref):
    for c in range(acc.shape[1] // LANES):
        sl = slice(c * LANES, (c + 1) * LANES)
        a = acc[:, sl]
        r = lax.rsqrt(jnp.mean(a * a, axis=-1, keepdims=True) + EPS)
        o_ref[:, sl] = (a * r * g_ref[:, sl]).astype(o_ref.dtype)


def _epi_rownorm(acc, g_ref, o_ref):
    r = lax.rsqrt(jnp.mean(acc * acc, axis=-1, keepdims=True) + EPS)
    o_ref[...] = (acc * r * g_ref[...]).astype(o_ref.dtype)


def _epi_ckv(acc, g_ref, ckv_ref, kr_ref, *, kv_rank):
    a = acc[:, :kv_rank]
    r = lax.rsqrt(jnp.mean(a * a, axis=-1, keepdims=True) + EPS)
    ckv_ref[...] = (a * r * g_ref[...]).astype(ckv_ref.dtype)
    kr_ref[...] = acc[:, kv_rank:]


def _epi_q_up(acc, c_ref, s_ref, gn_ref, gr_ref, gs_ref, o_ref, *, heads, scale):
    cg = c_ref[...] * gr_ref[...]
    sg = s_ref[...] * gs_ref[...]
    for h in range(heads):
        base = h * 3 * LANES
        nope = acc[:, base:base + LANES]
        rp = acc[:, base + LANES:base + 2 * LANES]
        sw = acc[:, base + 2 * LANES:base + 3 * LANES]
        ss = jnp.sum(nope * nope, axis=-1, keepdims=True) + jnp.sum(rp * rp, axis=-1, keepdims=True)
        r = lax.rsqrt(ss * (1.0 / MLA_QK_DIM) + EPS) * scale
        o_ref[h, :, 0:LANES] = (nope * r * gn_ref[...]).astype(o_ref.dtype)
        o_ref[h, :, LANES:2 * LANES] = ((rp * cg + sw * sg) * r).astype(o_ref.dtype)


def _ones_row_tile(cols, dtype):
    rows = lax.broadcasted_iota(jnp.int32, (BF16_SUBLANES, cols), 0)
    return jnp.where(rows == 0, 1.0, 0.0).astype(dtype)


def _epi_kv_up(acc, kr_ref, c_ref, s_ref, gn_ref, gr_ref, gs_ref, k_ref, vt_ref, *, heads):
    kr = kr_ref[:, 0:LANES]
    krs = kr_ref[:, LANES:2 * LANES]
    ssr = jnp.sum(kr * kr, axis=-1, keepdims=True)
    rope = kr * (c_ref[...] * gr_ref[...]) + krs * (s_ref[...] * gs_ref[...])
    for h in range(heads):
        base = h * 2 * LANES
        nope = acc[:, base:base + LANES]
        ss = jnp.sum(nope * nope, axis=-1, keepdims=True) + ssr
        r = lax.rsqrt(ss * (1.0 / MLA_QK_DIM) + EPS)
        k_ref[h, :, 0:LANES] = (nope * r * gn_ref[...]).astype(k_ref.dtype)
        k_ref[h, :, LANES:2 * LANES] = (rope * r).astype(k_ref.dtype)
        vt_ref[h, 0:MLA_V_DIM, :] = acc[:, base + LANES:base + 2 * LANES].T.astype(vt_ref.dtype)
        vt_ref[h, MLA_V_DIM:VT_ROWS, :] = _ones_row_tile(acc.shape[0], vt_ref.dtype)


def _epi_residual(acc, x_ref, g_ref, o_ref):
    o_ref[...] = x_ref[...] + g_ref[...] * acc


def _project(h, lw, tabs):
    m, d = h.shape
    bm = _tile(m, 1024)
    naw = NA_HEADS * NA_HEAD_DIM
    row = lambda n: pl.BlockSpec((1, n), lambda i, j: (0, 0))

    bn = _tile(2 * naw, 1024, LANES)
    na_qk = _matmul(h, lw["w_na_qk"], _epi_headnorm, [lw["g_na_qk"]],
                    [pl.BlockSpec((1, bn), lambda i, j: (0, j))],
                    jax.ShapeDtypeStruct((m, 2 * naw), BF16),
                    pl.BlockSpec((bm, bn), lambda i, j: (i, j)), bm, bn, "na_qk_proj")
    bn = _tile(naw, 1024, LANES)
    na_v = _matmul(h, lw["w_na_v"], _epi_cast, [], [],
                   jax.ShapeDtypeStruct((m, naw), BF16),
                   pl.BlockSpec((bm, bn), lambda i, j: (i, j)), bm, bn, "na_v_proj")

    q_rank = lw["w_cq"].n
    kv_rank = lw["w_ckv_kr"].n - 2 * LANES
    bmc = _tile(m, 512)
    cqn = _matmul(h, lw["w_cq"], _epi_rownorm, [lw["g_cq"]], [row(q_rank)],
                  jax.ShapeDtypeStruct((m, q_rank), BF16),
                  pl.BlockSpec((bmc, q_rank), lambda i, j: (i, 0)), bmc, q_rank, "cq_proj")
    ckvn, kr = _matmul(h, lw["w_ckv_kr"], functools.partial(_epi_ckv, kv_rank=kv_rank),
                       [lw["g_ckv"]], [row(kv_rank)],
                       (jax.ShapeDtypeStruct((m, kv_rank), BF16),
                        jax.ShapeDtypeStruct((m, 2 * LANES), F32)),
                       (pl.BlockSpec((bmc, kv_rank), lambda i, j: (i, 0)),
                        pl.BlockSpec((bmc, 2 * LANES), lambda i, j: (i, 0))),
                       bmc, kv_rank + 2 * LANES, "ckv_proj")

    cos_t, sin_t = tabs
    hb = 4 if MLA_HEADS % 4 == 0 else 1
    bmq = _tile(m, 512)
    tab = pl.BlockSpec((bmq, LANES), lambda i, j: (i, 0))
    vec = pl.BlockSpec((1, LANES), lambda i, j: (0, 0))
    mla_q = _matmul(cqn, lw["w_q_up"],
                    functools.partial(_epi_q_up, heads=hb, scale=MLA_QK_DIM ** -0.5 * LOG2E),
                    [cos_t, sin_t, lw["gq_nope"], lw["gq_rope"], lw["gq_swap"]],
                    [tab, tab, vec, vec, vec],
                    jax.ShapeDtypeStruct((MLA_HEADS, m, MLA_QK_PAD), BF16),
                    pl.BlockSpec((hb, bmq, MLA_QK_PAD), lambda i, j: (j, i, 0)),
                    bmq, hb * 3 * LANES, "mla_q_up")
    mla_k, mla_v = _matmul(ckvn, lw["w_kv_up"], functools.partial(_epi_kv_up, heads=hb),
                           [kr, cos_t, sin_t, lw["gk_nope"], lw["gk_rope"], lw["gk_swap"]],
                           [pl.BlockSpec((bmq, 2 * LANES), lambda i, j: (i, 0)), tab, tab, vec, vec, vec],
                           (jax.ShapeDtypeStruct((MLA_HEADS, m, MLA_QK_PAD), BF16),
                            jax.ShapeDtypeStruct((MLA_HEADS, VT_ROWS, m), BF16)),
                           (pl.BlockSpec((hb, bmq, MLA_QK_PAD), lambda i, j: (j, i, 0)),
                            pl.BlockSpec((hb, VT_ROWS, bmq), lambda i, j: (j, 0, i))),
                           bmq, hb * 2 * LANES, "mla_kv_up")
    return na_qk, na_v, mla_q, mla_k, mla_v


def _dot_nt(a, b):
    return lax.dot_general(a, b, (((1,), (1,)), ((), ())), preferred_element_type=F32)


def _softmax_pv(s_t, vt_blk, state):
    blk_max = jnp.max(s_t, axis=0, keepdims=True)
    m_new = blk_max if state is None else jnp.maximum(state[0], blk_max)
    p_t = jnp.exp2(s_t - m_new).astype(BF16)
    upd = jnp.dot(vt_blk, p_t, preferred_element_type=F32)
    if state is None:
        return m_new, upd
    return m_new, jnp.exp2(state[0] - m_new) * state[1] + upd


def _flash_kernel(*refs, tq, tk, n_chunks, has_ctx):
    if has_ctx:
        q_ref, k_ref, vt_ref, kc_ref, vct_ref, o_ref, s_scr = refs
    else:
        q_ref, k_ref, vt_ref, o_ref, s_scr = refs
    tile = functools.partial(_flash_tile, refs, tq=tq, tk=tk, n_chunks=n_chunks, has_ctx=has_ctx)
    n_tiles = q_ref.shape[0] // tq
    if n_tiles == 1:
        tile(0)
    else:
        lax.fori_loop(0, n_tiles, lambda t, carry: tile(t) or carry, 0)


def _flash_tile(refs, t, *, tq, tk, n_chunks, has_ctx):
    if has_ctx:
        q_ref, k_ref, vt_ref, kc_ref, vct_ref, o_ref, s_scr = refs
    else:
        q_ref, k_ref, vt_ref, o_ref, s_scr = refs
    rows = pl.ds(pl.multiple_of(t * tq, tq), tq)
    q = q_ref[rows, :]

    def chunk(c):
        return pl.ds(pl.multiple_of(c * tk, tk), tk)

    def scores(c, slot):
        s_scr[slot] = _dot_nt(k_ref[chunk(c), :], q)

    def consume(c, slot, state):
        return _softmax_pv(s_scr[slot], vt_ref[:, chunk(c)], state)

    if has_ctx:
        state = _softmax_pv(_dot_nt(kc_ref[...], q), vct_ref[...], None)
        first = 0
    else:
        state = _softmax_pv(_dot_nt(k_ref[0:tk, :], q), vt_ref[:, 0:tk], None)
        first = 1

    n_pairs = (n_chunks - first) // 2
    if n_pairs > 0:
        scores(first, 0)

        def pair(j, state, prefetch):
            c0 = first + 2 * j
            scores(c0 + 1, 1)
            state = consume(c0, 0, state)
            if prefetch:
                scores(c0 + 2, 0)
            return consume(c0 + 1, 1, state)

        state = lax.fori_loop(0, n_pairs - 1, lambda j, st: pair(j, st, True), state)
        state = pair(n_pairs - 1, state, False)
    if (n_chunks - first) % 2:
        c = n_chunks - 1
        state = _softmax_pv(_dot_nt(k_ref[c * tk:(c + 1) * tk, :], q), vt_ref[:, c * tk:(c + 1) * tk], state)

    acc = state[1]
    o_t = acc[0:MLA_V_DIM, :] / acc[MLA_V_DIM:MLA_V_DIM + 1, :]
    o_ref[rows, :] = o_t.T.astype(o_ref.dtype)


def _flash(q, k, vt, ctx_kv, *, name):
    heads, mq, dq = q.shape
    mk = k.shape[1]
    bq = _tile(mq, 2048)
    tq = _tile(bq, 512)
    tk = _tile(mk, 2048, LANES)
    q_spec = pl.BlockSpec((None, bq, dq), lambda h, i: (h, i, 0))
    whole = lambda a: pl.BlockSpec((None,) + a.shape[1:], lambda h, i: (h, 0, 0))
    in_specs = [q_spec, whole(k), whole(vt)]
    args = [q, k, vt]
    if ctx_kv is not None:
        in_specs += [whole(ctx_kv[0]), whole(ctx_kv[1])]
        args += list(ctx_kv)
    return pl.pallas_call(
        functools.partial(_flash_kernel, tq=tq, tk=tk, n_chunks=mk // tk, has_ctx=ctx_kv is not None),
        grid=(heads, mq // bq),
        in_specs=in_specs,
        out_specs=pl.BlockSpec((bq, MLA_V_DIM), lambda h, i: (i, h)),
        out_shape=jax.ShapeDtypeStruct((mq, heads * MLA_V_DIM), BF16),
        scratch_shapes=[pltpu.VMEM((2, tk, tq), F32)],
        compiler_params=_params("parallel", "parallel"),
        name=name,
    )(*args)


def _na_kernel(q_ref, k_ref, v_ref, kc_ref, vc_ref, b_ref, o_ref):
    head = lambda h: slice(h * NA_HEAD_DIM, (h + 1) * NA_HEAD_DIM)
    scores = [_dot_nt(q_ref[:, head(h)], jnp.concatenate([k_ref[:, head(h)], kc_ref[:, head(h)]], axis=0))
              + b_ref[0, h] for h in range(NA_HEADS)]
    probs, inv_l = [], []
    for s in scores:
        p = jnp.exp2(s - jnp.max(s, axis=-1, keepdims=True))
        inv_l.append(1.0 / jnp.sum(p, axis=-1, keepdims=True))
        probs.append(p.astype(BF16))
    outs = [jnp.dot(probs[h], jnp.concatenate([v_ref[:, head(h)], vc_ref[:, head(h)]], axis=0),
                    preferred_element_type=F32) * inv_l[h] for h in range(NA_HEADS)]
    o_ref[...] = jnp.concatenate(outs, axis=-1).astype(o_ref.dtype)


def _na_bias_table(rpb, n_ctx):
    qc = np.arange(GRID_W)
    kc = np.arange(GRID_W)
    cs = np.clip(qc - NA_WIN_W // 2, 0, GRID_W - NA_WIN_W)
    col_ok = (kc[None, :] >= cs[:, None]) & (kc[None, :] < cs[:, None] + NA_WIN_W)
    col_idx = np.clip(kc[None, :] - qc[:, None] + NA_WIN_W - 1, 0, 2 * NA_WIN_W - 2)
    cols = rpb.astype(F32)[:, :, col_idx]
    cols = jnp.where(jnp.asarray(col_ok)[None, None], cols * LOG2E, NEG_INF)
    variants = [cols[:, v:v + NA_WIN_H].transpose(0, 2, 1, 3).reshape(NA_HEADS, GRID_W, NA_WIN_H * GRID_W)
                for v in range(NA_WIN_H)]
    table = jnp.stack(variants)
    return jnp.concatenate([table, jnp.zeros(table.shape[:3] + (n_ctx,), F32)], axis=-1)


def _na_attention(na_qk, na_v, ctx_qk, ctx_v, bias):
    m = na_qk.shape[0]
    naw = NA_HEADS * NA_HEAD_DIM
    rows_n = m // GRID_W
    win = NA_WIN_H * GRID_W
    n_ctx = ctx_qk.shape[0]

    def start(r):
        return jnp.clip(r - NA_WIN_H // 2, 0, rows_n - NA_WIN_H)

    E = pl.Element
    return pl.pallas_call(
        _na_kernel,
        grid=(rows_n,),
        in_specs=[
            pl.BlockSpec((GRID_W, naw), lambda r: (r, 0)),
            pl.BlockSpec((E(win), E(naw)), lambda r: (start(r) * GRID_W, naw)),
            pl.BlockSpec((E(win), E(naw)), lambda r: (start(r) * GRID_W, 0)),
            pl.BlockSpec((n_ctx, naw), lambda r: (0, 1)),
            pl.BlockSpec((n_ctx, naw), lambda r: (0, 0)),
            pl.BlockSpec((1, NA_HEADS, GRID_W, win + n_ctx), lambda r: (start(r) - r + NA_WIN_H - 1, 0, 0, 0)),
        ],
        out_specs=pl.BlockSpec((GRID_W, naw), lambda r: (r, 0)),
        out_shape=jax.ShapeDtypeStruct((m, naw), BF16),
        compiler_params=_params("parallel"),
        name="na_attention",
    )(na_qk, na_qk, na_v, ctx_qk, ctx_v, bias)


def _merge_kernel(h_ref, yn_ref, ym_ref, wgn_ref, wgm_ref, wbn_ref, wbm_ref, o_ref):
    h = h_ref[...]
    gn = jax.nn.sigmoid(jnp.dot(h, wgn_ref[...], preferred_element_type=F32))
    gm = jax.nn.sigmoid(jnp.dot(h, wgm_ref[...], preferred_element_type=F32))
    a = jnp.dot(yn_ref[...], wbn_ref[...], preferred_element_type=F32)
    b = jnp.dot(ym_ref[...], wbm_ref[...], preferred_element_type=F32)
    o_ref[...] = (gn * a + gm * b).astype(o_ref.dtype)


def _merge(h, y_na, y_mla, lw):
    m, d = h.shape
    bm = _tile(m, 512)
    bn = _tile(d, 512, LANES)
    nj = d // bn
    kn, km = y_na.shape[1], y_mla.shape[1]
    return pl.pallas_call(
        _merge_kernel,
        grid=(m // bm, nj),
        in_specs=[pl.BlockSpec((bm, d), lambda i, j: (i, 0)),
                  pl.BlockSpec((bm, kn), lambda i, j: (i, 0)),
                  pl.BlockSpec((bm, km), lambda i, j: (i, 0)),
                  _w_spec(lw["w_gates"], bn),
                  _w_spec(lw["w_gates"], bn, shift=nj),
                  _w_spec(lw["w_b_na"], bn),
                  _w_spec(lw["w_b_mla"], bn)],
        out_specs=pl.BlockSpec((bm, bn), lambda i, j: (i, j)),
        out_shape=jax.ShapeDtypeStruct((m, d), BF16),
        compiler_params=_params("parallel", "parallel"),
        name="branch_merge",
    )(h, y_na, y_mla, lw["w_gates"].arr, lw["w_gates"].arr, lw["w_b_na"].arr, lw["w_b_mla"].arr)


def _residual_matmul(a, w, x, gate, name):
    m, d = x.shape
    bm = _tile(m, 512)
    bn = _tile(d, 1024, LANES)
    return _matmul(a, w, _epi_residual, [x, gate.reshape(1, d)],
                   [pl.BlockSpec((bm, bn), lambda i, j: (i, j)),
                    pl.BlockSpec((1, bn), lambda i, j: (0, j))],
                   jax.ShapeDtypeStruct((m, d), F32),
                   pl.BlockSpec((bm, bn), lambda i, j: (i, j)), bm, bn, name)


def _ffn_up_kernel(x_ref, wg_ref, wv_ref, halo_ref, cw_ref, cb_ref, o_ref):
    x = x_ref[...]
    g = jnp.dot(x, wg_ref[...], preferred_element_type=F32)
    bm = g.shape[0]
    rows = lax.broadcasted_iota(jnp.int32, g.shape, 0)
    g_prev = jnp.where(rows == 0, halo_ref[0:1, :], pltpu.roll(g, 1, axis=0))
    g_next = jnp.where(rows == bm - 1, halo_ref[1:2, :], pltpu.roll(g, bm - 1, axis=0))
    y = g_prev * cw_ref[0:1, :] + g * cw_ref[1:2, :] + g_next * cw_ref[2:3, :] + cb_ref[...]
    v = jnp.dot(x, wv_ref[...], preferred_element_type=F32)
    o_ref[...] = (y * jax.nn.sigmoid(y) * v).astype(o_ref.dtype)


def _ffn_up(h2, lw):
    m, d = h2.shape
    dff = lw["conv_b"].shape[1]
    bm = _tile(m, 1024)
    bn = _tile(dff, 512, LANES)
    nt, nj = m // bm, dff // bn

    edge = h2.reshape(nt, bm, d)[:, (0, bm - 1), :].reshape(2 * nt, d)
    pad = (-edge.shape[0]) % 16
    edge = jnp.pad(edge, ((0, pad), (0, 0)))
    bne = _tile(dff, 1024, LANES)
    eg = _matmul(edge, lw["w_up"]._replace(n=dff), _epi_cast, [], [],
                 jax.ShapeDtypeStruct((edge.shape[0], dff), F32),
                 pl.BlockSpec((edge.shape[0], bne), lambda i, j: (i, j)),
                 edge.shape[0], bne, "ffn_edge_rows")
    eg = eg[:2 * nt].reshape(nt, 2, dff)
    zero = jnp.zeros((1, dff), F32)
    prev_rows = jnp.concatenate([zero, eg[:-1, 1]], axis=0)
    next_rows = jnp.concatenate([eg[1:, 0], zero], axis=0)
    halo = jnp.stack([prev_rows, next_rows], axis=1)

    return pl.pallas_call(
        _ffn_up_kernel,
        grid=(m // bm, nj),
        in_specs=[pl.BlockSpec((bm, d), lambda i, j: (i, 0)),
                  _w_spec(lw["w_up"], bn),
                  _w_spec(lw["w_up"], bn, shift=nj),
                  pl.BlockSpec((None, 2, bn), lambda i, j: (i, 0, j)),
                  pl.BlockSpec((CONV_W, bn), lambda i, j: (0, j)),
                  pl.BlockSpec((1, bn), lambda i, j: (0, j))],
        out_specs=pl.BlockSpec((bm, bn), lambda i, j: (i, j)),
        out_shape=jax.ShapeDtypeStruct((m, dff), BF16),
        compiler_params=_params("parallel", "parallel"),
        name="ffn_up_conv",
    )(h2, lw["w_up"].arr, lw["w_up"].arr, halo, lw["conv_w"], lw["conv_b"])


def _rope_partner():
    half = MLA_ROPE_DIM // 2
    quarter = half // 2
    idx = np.arange(MLA_ROPE_DIM)
    return np.where((idx % half) < quarter, idx + quarter, idx - quarter)


def _pad_lanes(a):
    return jnp.pad(a, [(0, 0)] * (a.ndim - 1) + [(0, LANES - a.shape[-1])])


def _stacked_weights(w_in, mla_w_q_up, mla_w_kv_up, w_branch_na, w_branch_mla, w_out, ffn_w_up, ffn_w_down):
    depth = w_in.shape[0]
    naw = NA_HEADS * NA_HEAD_DIM
    q_rank = mla_w_q_up.shape[1]
    kv_rank = mla_w_kv_up.shape[1]
    perm = _rope_partner()
    o_cq = 3 * naw
    o_ckv = o_cq + q_rank
    o_kr = o_ckv + kv_rank
    o_g = o_kr + MLA_ROPE_DIM
    w_kr = w_in[:, :, o_kr:o_g]
    wq = mla_w_q_up.reshape(depth, q_rank, MLA_HEADS, MLA_QK_DIM)
    wq_rope = wq[..., MLA_NOPE_DIM:]
    return {
        "in_head": w_in[:, :, :o_ckv].astype(BF16),
        "ckv_kr": jnp.concatenate([w_in[:, :, o_ckv:o_kr], _pad_lanes(w_kr), _pad_lanes(w_kr[:, :, perm])],
                                  axis=2).astype(BF16),
        "gates": w_in[:, :, o_g:].astype(BF16),
        "q_up": jnp.concatenate([wq[..., :MLA_NOPE_DIM], _pad_lanes(wq_rope), _pad_lanes(wq_rope[..., perm])],
                                axis=3).reshape(depth, q_rank, MLA_HEADS * 3 * LANES).astype(BF16),
        "kv_up": mla_w_kv_up.astype(BF16),
        "b_na": w_branch_na.astype(BF16),
        "b_mla": w_branch_mla.astype(BF16),
        "out": w_out.astype(BF16),
        "up": ffn_w_up.astype(BF16),
        "down": ffn_w_down.astype(BF16),
    }


def _layer_weights(l, sw, na_q_norm, na_k_norm, mla_cq_norm, mla_ckv_norm, mla_q_norm, mla_k_norm,
                   ffn_conv_w, ffn_conv_b):
    naw = NA_HEADS * NA_HEAD_DIM
    q_rank = sw["q_up"].shape[1]
    kv_rank = sw["kv_up"].shape[1]
    perm = _rope_partner()
    whole = lambda a: _W(a, l, 0, a.shape[2])
    lw = {
        "w_na_qk": _W(sw["in_head"], l, 0, 2 * naw),
        "w_na_v": _W(sw["in_head"], l, 2 * naw, naw),
        "w_cq": _W(sw["in_head"], l, 3 * naw, q_rank),
        "w_ckv_kr": whole(sw["ckv_kr"]),
        "w_gates": whole(sw["gates"]),
        "w_q_up": whole(sw["q_up"]),
        "w_kv_up": whole(sw["kv_up"]),
        "w_b_na": whole(sw["b_na"]),
        "w_b_mla": whole(sw["b_mla"]),
        "w_out": whole(sw["out"]),
        "w_up": whole(sw["up"]),
        "w_down": whole(sw["down"]),
        "g_na_qk": jnp.concatenate([jnp.tile(na_q_norm[l] * (NA_HEAD_DIM ** -0.5 * LOG2E), NA_HEADS),
                                    jnp.tile(na_k_norm[l], NA_HEADS)]).reshape(1, 2 * naw),
        "g_cq": mla_cq_norm[l].reshape(1, q_rank),
        "g_ckv": mla_ckv_norm[l].reshape(1, kv_rank),
        "conv_w": ffn_conv_w[l],
        "conv_b": ffn_conv_b[l].reshape(1, -1),
    }
    for tag, g in (("q", mla_q_norm[l]), ("k", mla_k_norm[l])):
        g_rope = g[MLA_NOPE_DIM:]
        lw[f"g{tag}_nope"] = g[:MLA_NOPE_DIM].reshape(1, LANES)
        lw[f"g{tag}_rope"] = _pad_lanes(g_rope).reshape(1, LANES)
        lw[f"g{tag}_swap"] = _pad_lanes(g_rope[perm]).reshape(1, LANES)
    return lw


def _rope_tables(n_tok):
    t = jnp.arange(n_tok, dtype=jnp.int32)
    half = MLA_ROPE_DIM // 2
    inv_freq = ROPE_THETA ** (-jnp.arange(0, half, 2, dtype=F32) / half)
    cs, sn = [], []
    for pos in (t // GRID_W, t % GRID_W):
        ang = pos.astype(F32)[:, None] * inv_freq[None, :]
        cs += [jnp.cos(ang), jnp.cos(ang)]
        sn += [-jnp.sin(ang), jnp.sin(ang)]
    return _pad_lanes(jnp.concatenate(cs, axis=1)), _pad_lanes(jnp.concatenate(sn, axis=1))


def _identity_tables(n_tok):
    cos_t = _pad_lanes(jnp.ones((n_tok, MLA_ROPE_DIM), F32))
    return cos_t, jnp.zeros((n_tok, LANES), F32)


def _head_major(a, heads):
    rows = a.shape[0]
    return a.reshape(rows, heads, -1).transpose(1, 0, 2)


def _head_major_vt(v, heads):
    rows = v.shape[0]
    vt = v.reshape(rows, heads, MLA_V_DIM).transpose(1, 2, 0)
    tail = jnp.zeros((heads, BF16_SUBLANES, rows), v.dtype).at[:, 0, :].set(1.0)
    return jnp.concatenate([vt, tail], axis=1)


def kernel(x, c, ctx, c_ctx, ada_down, ada_up, ada_bias, norm_mix, norm_ffn, w_in, na_q_norm, na_k_norm, na_rpb, mla_cq_norm, mla_ckv_norm, mla_w_q_up, mla_w_kv_up, mla_q_norm, mla_k_norm, w_branch_na, w_branch_mla, w_out, ffn_w_up, ffn_conv_w, ffn_conv_b, ffn_w_down):
    batch, n_tok, d = x.shape
    assert batch == 1 and c.shape[0] == 1
    depth = w_in.shape[0]
    naw = NA_HEADS * NA_HEAD_DIM
    xs = x[0]
    cs = ctx[0]
    lat_tabs = _rope_tables(n_tok)
    ctx_tabs = _identity_tables(cs.shape[0])
    cond = jnp.zeros((8, d), F32).at[0].set(c[0]).at[1].set(c_ctx)
    sw = _stacked_weights(w_in, mla_w_q_up, mla_w_kv_up, w_branch_na, w_branch_mla, w_out, ffn_w_up, ffn_w_down)

    for l in range(depth):
        last = l == depth - 1
        lw = _layer_weights(l, sw, na_q_norm, na_k_norm, mla_cq_norm, mla_ckv_norm, mla_q_norm, mla_k_norm,
                            ffn_conv_w, ffn_conv_b)
        mods = _adaln(cond, ada_down, ada_up, ada_bias, l)
        sh1, sc1, g1, sh2, sc2, g2 = [mods[0, i * d:(i + 1) * d] for i in range(N_MOD)]
        csh1, csc1, cg1, csh2, csc2, cg2 = [mods[1, i * d:(i + 1) * d] for i in range(N_MOD)]

        h = _norm_mod(xs, norm_mix[l], sc1, sh1)
        hc = _norm_mod(cs, norm_mix[l], csc1, csh1)
        na_qk, na_v, mq, mk, mv = _project(h, lw, lat_tabs)
        c_qk, c_v, cmq, cmk, cmv = _project(hc, lw, ctx_tabs)

        y_na = _na_attention(na_qk, na_v, c_qk, c_v, _na_bias_table(na_rpb[l], cs.shape[0]))
        y_mla = _flash(mq, mk, mv, (cmk, cmv), name="mla_attention")
        u = _merge(h, y_na, y_mla, lw)
        xs = _residual_matmul(u, lw["w_out"], xs, g1, "out_proj")
        h2 = _norm_mod(xs, norm_ffn[l], sc2, sh2)
        xs = _residual_matmul(_ffn_up(h2, lw), lw["w_down"], xs, g2, "ffn_down")

        if not last:
            yc_na = _flash(_head_major(c_qk[:, :naw], NA_HEADS), _head_major(c_qk[:, naw:], NA_HEADS),
                           _head_major_vt(c_v, NA_HEADS), None, name="ctx_na_attention")
            yc_mla = _flash(cmq, cmk, cmv, None, name="ctx_mla_attention")
            uc = _merge(hc, yc_na, yc_mla, lw)
            cs = _residual_matmul(uc, lw["w_out"], cs, cg1, "ctx_out_proj")
            hc2 = _norm_mod(cs, norm_ffn[l], csc2, csh2)
            cs = _residual_matmul(_ffn_up(hc2, lw), lw["w_down"], cs, cg2, "ctx_ffn_down")
    return xs[None]
```

```python
import functools
from typing import NamedTuple

import numpy as np
import jax
import jax.numpy as jnp
from jax import lax
from jax.experimental import pallas as pl
from jax.experimental.pallas import tpu as pltpu

F32 = jnp.float32
BF16 = jnp.bfloat16

GRID_W = 64
EPS = 1e-6
NEG_INF = -1e30
N_MOD = 6

NA_HEADS = 16
NA_HEAD_DIM = 128
NA_WIN_H = 8
NA_WIN_W = 16

MLA_HEADS = 16
MLA_NOPE_DIM = 128
MLA_ROPE_DIM = 64
MLA_QK_DIM = MLA_NOPE_DIM + MLA_ROPE_DIM
MLA_V_DIM = 128
ROPE_THETA = 10000.0
CONV_W = 3

LANES = 128
MLA_QK_PAD = 2 * LANES
BF16_SUBLANES = 16
VT_ROWS = MLA_V_DIM + BF16_SUBLANES
LOG2E = 1.4426950408889634
VMEM_LIMIT_BYTES = 56 * 1024 * 1024


def _tile(dim, pref, mult=8):
    if dim <= pref:
        return dim
    t = (pref // mult) * mult
    while t >= mult:
        if dim % t == 0:
            return t
        t -= mult
    return dim


def _params(*sem):
    return pltpu.CompilerParams(dimension_semantics=sem, vmem_limit_bytes=VMEM_LIMIT_BYTES)


def _ada_down_kernel(c_ref, w_ref, o_ref):
    c = c_ref[...]
    a = (c * jax.nn.sigmoid(c)).astype(BF16)
    o_ref[...] = jnp.dot(a, w_ref[...].astype(BF16), preferred_element_type=F32)


def _ada_up_kernel(t_ref, w_ref, b_ref, o_ref):
    o_ref[...] = jnp.dot(t_ref[...].astype(BF16), w_ref[...].astype(BF16),
                         preferred_element_type=F32) + b_ref[...]


def _adaln(cond, down, up, bias, layer):
    rows, d = cond.shape
    rank = down.shape[2]
    n = up.shape[2]
    bn1 = _tile(rank, 512, LANES)
    t = pl.pallas_call(
        _ada_down_kernel,
        grid=(rank // bn1,),
        in_specs=[pl.BlockSpec((rows, d), lambda j: (0, 0)),
                  pl.BlockSpec((None, d, bn1), lambda j: (layer, 0, j))],
        out_specs=pl.BlockSpec((rows, bn1), lambda j: (0, j)),
        out_shape=jax.ShapeDtypeStruct((rows, rank), F32),
        compiler_params=_params("parallel"),
        name="ada_down",
    )(cond, down)
    bn2 = _tile(n, 2048, LANES)
    return pl.pallas_call(
        _ada_up_kernel,
        grid=(n // bn2,),
        in_specs=[pl.BlockSpec((rows, rank), lambda j: (0, 0)),
                  pl.BlockSpec((None, rank, bn2), lambda j: (layer, 0, j)),
                  pl.BlockSpec((None, 1, bn2), lambda j: (layer, 0, j))],
        out_specs=pl.BlockSpec((rows, bn2), lambda j: (0, j)),
        out_shape=jax.ShapeDtypeStruct((rows, n), F32),
        compiler_params=_params("parallel"),
        name="ada_up",
    )(t, up, bias.reshape(bias.shape[0], 1, n))


def _norm_mod_kernel(x_ref, g_ref, sc_ref, sh_ref, o_ref):
    x = x_ref[...]
    y = x * lax.rsqrt(jnp.mean(x * x, axis=-1, keepdims=True) + EPS) * g_ref[...]
    o_ref[...] = (y * (1.0 + sc_ref[...]) + sh_ref[...]).astype(o_ref.dtype)


def _norm_mod(x, g, sc, sh):
    m, d = x.shape
    bm = _tile(m, 512)
    vec = pl.BlockSpec((1, d), lambda i: (0, 0))
    return pl.pallas_call(
        _norm_mod_kernel,
        grid=(m // bm,),
        in_specs=[pl.BlockSpec((bm, d), lambda i: (i, 0)), vec, vec, vec],
        out_specs=pl.BlockSpec((bm, d), lambda i: (i, 0)),
        out_shape=jax.ShapeDtypeStruct((m, d), BF16),
        compiler_params=_params("parallel"),
        name="norm_mod",
    )(x, g.reshape(1, d), sc.reshape(1, d), sh.reshape(1, d))


def _mm_kernel(x_ref, w_ref, *rest, epilogue):
    acc = jnp.dot(x_ref[...], w_ref[...], preferred_element_type=F32)
    epilogue(acc, *rest)


class _W(NamedTuple):
    arr: jax.Array
    layer: int
    col0: int
    n: int


def _w_spec(w, bn, shift=0):
    blk0, rem = divmod(w.col0, bn)
    assert rem == 0 and w.n % bn == 0
    return pl.BlockSpec((None, w.arr.shape[1], bn), lambda i, j: (w.layer, 0, blk0 + shift + j))


def _matmul(x, w, epilogue, extras, extra_specs, out_shapes, out_specs, bm, bn, name):
    m, k = x.shape
    return pl.pallas_call(
        functools.partial(_mm_kernel, epilogue=epilogue),
        grid=(m // bm, w.n // bn),
        in_specs=[pl.BlockSpec((bm, k), lambda i, j: (i, 0)), _w_spec(w, bn)] + list(extra_specs),
        out_specs=out_specs,
        out_shape=out_shapes,
        compiler_params=_params("parallel", "parallel"),
        name=name,
    )(x, w.arr, *extras)


def _epi_cast(acc, o_ref):
    o_ref[...] = acc.astype(o_ref.dtype)


def _epi_headnorm(acc, g_ref, o_ref):
    for c in range(acc.shape[1] // LANES):
        sl = slice(c * LANES, (c + 1) * LANES)
        a = acc[:, sl]
        r = lax.rsqrt(jnp.mean(a * a, axis=-1, keepdims=True) + EPS)
        o_ref[:, sl] = (a * r * g_ref[:, sl]).astype(o_ref.dtype)


def _epi_rownorm(acc, g_ref, o_ref):
    r = lax.rsqrt(jnp.mean(acc * acc, axis=-1, keepdims=True) + EPS)
    o_ref[...] = (acc * r * g_ref[...]).astype(o_ref.dtype)


def _epi_ckv(acc, g_ref, ckv_ref, kr_ref, *, kv_rank):
    a = acc[:, :kv_rank]
    r = lax.rsqrt(jnp.mean(a * a, axis=-1, keepdims=True) + EPS)
    ckv_ref[...] = (a * r * g_ref[...]).astype(ckv_ref.dtype)
    kr_ref[...] = acc[:, kv_rank:]


def _epi_q_up(acc, c_ref, s_ref, gn_ref, gr_ref, gs_ref, o_ref, *, heads, scale):
    cg = c_ref[...] * gr_ref[...]
    sg = s_ref[...] * gs_ref[...]
    for h in range(heads):
        base = h * 3 * LANES
        nope = acc[:, base:base + LANES]
        rp = acc[:, base + LANES:base + 2 * LANES]
        sw = acc[:, base + 2 * LANES:base + 3 * LANES]
        ss = jnp.sum(nope * nope, axis=-1, keepdims=True) + jnp.sum(rp * rp, axis=-1, keepdims=True)
        r = lax.rsqrt(ss * (1.0 / MLA_QK_DIM) + EPS) * scale
        o_ref[h, :, 0:LANES] = (nope * r * gn_ref[...]).astype(o_ref.dtype)
        o_ref[h, :, LANES:2 * LANES] = ((rp * cg + sw * sg) * r).astype(o_ref.dtype)


def _ones_row_tile(cols, dtype):
    rows = lax.broadcasted_iota(jnp.int32, (BF16_SUBLANES, cols), 0)
    return jnp.where(rows == 0, 1.0, 0.0).astype(dtype)


def _epi_kv_up(acc, kr_ref, c_ref, s_ref, gn_ref, gr_ref, gs_ref, k_ref, vt_ref, *, heads):
    kr = kr_ref[:, 0:LANES]
    krs = kr_ref[:, LANES:2 * LANES]
    ssr = jnp.sum(kr * kr, axis=-1, keepdims=True)
    rope = kr * (c_ref[...] * gr_ref[...]) + krs * (s_ref[...] * gs_ref[...])
    for h in range(heads):
        base = h * 2 * LANES
        nope = acc[:, base:base + LANES]
        ss = jnp.sum(nope * nope, axis=-1, keepdims=True) + ssr
        r = lax.rsqrt(ss * (1.0 / MLA_QK_DIM) + EPS)
        k_ref[h, :, 0:LANES] = (nope * r * gn_ref[...]).astype(k_ref.dtype)
        k_ref[h, :, LANES:2 * LANES] = (rope * r).astype(k_ref.dtype)
        vt_ref[h, 0:MLA_V_DIM, :] = acc[:, base + LANES:base + 2 * LANES].T.astype(vt_ref.dtype)
        vt_ref[h, MLA_V_DIM:VT_ROWS, :] = _ones_row_tile(acc.shape[0], vt_ref.dtype)


def _epi_residual(acc, x_ref, g_ref, o_ref):
    o_ref[...] = x_ref[...] + g_ref[...] * acc


def _project(h, lw, tabs):
    m, d = h.shape
    bm = _tile(m, 1024)
    naw = NA_HEADS * NA_HEAD_DIM
    row = lambda n: pl.BlockSpec((1, n), lambda i, j: (0, 0))

    bn = _tile(2 * naw, 1024, LANES)
    na_qk = _matmul(h, lw["w_na_qk"], _epi_headnorm, [lw["g_na_qk"]],
                    [pl.BlockSpec((1, bn), lambda i, j: (0, j))],
                    jax.ShapeDtypeStruct((m, 2 * naw), BF16),
                    pl.BlockSpec((bm, bn), lambda i, j: (i, j)), bm, bn, "na_qk_proj")
    bn = _tile(naw, 1024, LANES)
    na_v = _matmul(h, lw["w_na_v"], _epi_cast, [], [],
                   jax.ShapeDtypeStruct((m, naw), BF16),
                   pl.BlockSpec((bm, bn), lambda i, j: (i, j)), bm, bn, "na_v_proj")

    q_rank = lw["w_cq"].n
    kv_rank = lw["w_ckv_kr"].n - 2 * LANES
    bmc = _tile(m, 512)
    cqn = _matmul(h, lw["w_cq"], _epi_rownorm, [lw["g_cq"]], [row(q_rank)],
                  jax.ShapeDtypeStruct((m, q_rank), BF16),
                  pl.BlockSpec((bmc, q_rank), lambda i, j: (i, 0)), bmc, q_rank, "cq_proj")
    ckvn, kr = _matmul(h, lw["w_ckv_kr"], functools.partial(_epi_ckv, kv_rank=kv_rank),
                       [lw["g_ckv"]], [row(kv_rank)],
                       (jax.ShapeDtypeStruct((m, kv_rank), BF16),
                        jax.ShapeDtypeStruct((m, 2 * LANES), F32)),
                       (pl.BlockSpec((bmc, kv_rank), lambda i, j: (i, 0)),
                        pl.BlockSpec((bmc, 2 * LANES), lambda i, j: (i, 0))),
                       bmc, kv_rank + 2 * LANES, "ckv_proj")

    cos_t, sin_t = tabs
    hb = 4 if MLA_HEADS % 4 == 0 else 1
    bmq = _tile(m, 512)
    tab = pl.BlockSpec((bmq, LANES), lambda i, j: (i, 0))
    vec = pl.BlockSpec((1, LANES), lambda i, j: (0, 0))
    mla_q = _matmul(cqn, lw["w_q_up"],
                    functools.partial(_epi_q_up, heads=hb, scale=MLA_QK_DIM ** -0.5 * LOG2E),
                    [cos_t, sin_t, lw["gq_nope"], lw["gq_rope"], lw["gq_swap"]],
                    [tab, tab, vec, vec, vec],
                    jax.ShapeDtypeStruct((MLA_HEADS, m, MLA_QK_PAD), BF16),
                    pl.BlockSpec((hb, bmq, MLA_QK_PAD), lambda i, j: (j, i, 0)),
                    bmq, hb * 3 * LANES, "mla_q_up")
    mla_k, mla_v = _matmul(ckvn, lw["w_kv_up"], functools.partial(_epi_kv_up, heads=hb),
                           [kr, cos_t, sin_t, lw["gk_nope"], lw["gk_rope"], lw["gk_swap"]],
                           [pl.BlockSpec((bmq, 2 * LANES), lambda i, j: (i, 0)), tab, tab, vec, vec, vec],
                           (jax.ShapeDtypeStruct((MLA_HEADS, m, MLA_QK_PAD), BF16),
                            jax.ShapeDtypeStruct((MLA_HEADS, VT_ROWS, m), BF16)),
                           (pl.BlockSpec((hb, bmq, MLA_QK_PAD), lambda i, j: (j, i, 0)),
                            pl.BlockSpec((hb, VT_ROWS, bmq), lambda i, j: (j, 0, i))),
                           bmq, hb * 2 * LANES, "mla_kv_up")
    return na_qk, na_v, mla_q, mla_k, mla_v


def _dot_nt(a, b):
    return lax.dot_general(a, b, (((1,), (1,)), ((), ())), preferred_element_type=F32)


def _softmax_pv(s_t, vt_blk, state):
    blk_max = jnp.max(s_t, axis=0, keepdims=True)
    m_new = blk_max if state is None else jnp.maximum(state[0], blk_max)
    p_t = jnp.exp2(s_t - m_new).astype(BF16)
    upd = jnp.dot(vt_blk, p_t, preferred_element_type=F32)
    if state is None:
        return m_new, upd
    return m_new, jnp.exp2(state[0] - m_new) * state[1] + upd


def _flash_kernel(*refs, tq, tk, n_chunks, has_ctx):
    if has_ctx:
        q_ref, k_ref, vt_ref, kc_ref, vct_ref, o_ref, s_scr = refs
    else:
        q_ref, k_ref, vt_ref, o_ref, s_scr = refs
    n_sub = q_ref.shape[0] // tq
    sub_rows = [slice(i * tq, (i + 1) * tq) for i in range(n_sub)]
    qs = [q_ref[r, :] for r in sub_rows]

    def chunk(c):
        return pl.ds(pl.multiple_of(c * tk, tk), tk)

    def scores(c, slot):
        k_blk = k_ref[chunk(c), :]
        for i, q in enumerate(qs):
            s_scr[slot, i, :, 0:tq] = _dot_nt(k_blk, q)

    def consume(c, slot, state):
        vt_blk = vt_ref[:, chunk(c)]
        return tuple(_softmax_pv(s_scr[slot, i, :, 0:tq], vt_blk, st) for i, st in enumerate(state))

    first = 0 if has_ctx else 1
    n_pairs = (n_chunks - first) // 2
    if has_ctx:
        init_scores = [_dot_nt(kc_ref[...], q) for q in qs]
        init_vt = vct_ref[...]
    else:
        init_scores = [_dot_nt(k_ref[0:tk, :], q) for q in qs]
        init_vt = vt_ref[:, 0:tk]
    if n_pairs > 0:
        scores(first, 0)
    state = tuple(_softmax_pv(s, init_vt, None) for s in init_scores)

    if n_pairs > 0:

        def pair(j, state, prefetch):
            c0 = first + 2 * j
            scores(c0 + 1, 1)
            state = consume(c0, 0, state)
            if prefetch:
                scores(c0 + 2, 0)
            return consume(c0 + 1, 1, state)

        state = lax.fori_loop(0, n_pairs - 1, lambda j, st: pair(j, st, True), state)
        state = pair(n_pairs - 1, state, False)
    if (n_chunks - first) % 2:
        last = slice((n_chunks - 1) * tk, n_chunks * tk)
        state = tuple(_softmax_pv(_dot_nt(k_ref[last, :], q), vt_ref[:, last], st) for q, st in zip(qs, state))

    for rows, (_, acc) in zip(sub_rows, state):
        o_t = acc[0:MLA_V_DIM, :] / acc[MLA_V_DIM:MLA_V_DIM + 1, :]
        o_ref[rows, :] = o_t.T.astype(o_ref.dtype)


def _flash(q, k, vt, ctx_kv, *, name):
    heads, mq, dq = q.shape
    mk = k.shape[1]
    bq = _tile(mq, 512)
    tq = _tile(bq, 512)
    tk = _tile(mk, 2048, LANES)
    q_spec = pl.BlockSpec((None, bq, dq), lambda h, i: (h, i, 0))
    whole = lambda a: pl.BlockSpec((None,) + a.shape[1:], lambda h, i: (h, 0, 0))
    in_specs = [q_spec, whole(k), whole(vt)]
    args = [q, k, vt]
    if ctx_kv is not None:
        in_specs += [whole(ctx_kv[0]), whole(ctx_kv[1])]
        args += list(ctx_kv)
    return pl.pallas_call(
        functools.partial(_flash_kernel, tq=tq, tk=tk, n_chunks=mk // tk, has_ctx=ctx_kv is not None),
        grid=(heads, mq // bq),
        in_specs=in_specs,
        out_specs=pl.BlockSpec((bq, MLA_V_DIM), lambda h, i: (i, h)),
        out_shape=jax.ShapeDtypeStruct((mq, heads * MLA_V_DIM), BF16),
        scratch_shapes=[pltpu.VMEM((2, bq // tq, tk, tq + LANES), F32)],
        compiler_params=_params("parallel", "parallel"),
        name=name,
    )(*args)


def _na_kernel(q_ref, k_ref, v_ref, kc_ref, vc_ref, b_ref, o_ref):
    head = lambda h: slice(h * NA_HEAD_DIM, (h + 1) * NA_HEAD_DIM)
    scores = [_dot_nt(q_ref[:, head(h)], jnp.concatenate([k_ref[:, head(h)], kc_ref[:, head(h)]], axis=0))
              + b_ref[0, h] for h in range(NA_HEADS)]
    probs, inv_l = [], []
    for s in scores:
        p = jnp.exp2(s - jnp.max(s, axis=-1, keepdims=True))
        inv_l.append(1.0 / jnp.sum(p, axis=-1, keepdims=True))
        probs.append(p.astype(BF16))
    outs = [jnp.dot(probs[h], jnp.concatenate([v_ref[:, head(h)], vc_ref[:, head(h)]], axis=0),
                    preferred_element_type=F32) * inv_l[h] for h in range(NA_HEADS)]
    o_ref[...] = jnp.concatenate(outs, axis=-1).astype(o_ref.dtype)


def _na_bias_table(rpb, n_ctx):
    qc = np.arange(GRID_W)
    kc = np.arange(GRID_W)
    cs = np.clip(qc - NA_WIN_W // 2, 0, GRID_W - NA_WIN_W)
    col_ok = (kc[None, :] >= cs[:, None]) & (kc[None, :] < cs[:, None] + NA_WIN_W)
    col_idx = np.clip(kc[None, :] - qc[:, None] + NA_WIN_W - 1, 0, 2 * NA_WIN_W - 2)
    cols = rpb.astype(F32)[:, :, col_idx]
    cols = jnp.where(jnp.asarray(col_ok)[None, None], cols * LOG2E, NEG_INF)
    variants = [cols[:, v:v + NA_WIN_H].transpose(0, 2, 1, 3).reshape(NA_HEADS, GRID_W, NA_WIN_H * GRID_W)
                for v in range(NA_WIN_H)]
    table = jnp.stack(variants)
    return jnp.concatenate([table, jnp.zeros(table.shape[:3] + (n_ctx,), F32)], axis=-1)


def _na_attention(na_qk, na_v, ctx_qk, ctx_v, bias):
    m = na_qk.shape[0]
    naw = NA_HEADS * NA_HEAD_DIM
    rows_n = m // GRID_W
    win = NA_WIN_H * GRID_W
    n_ctx = ctx_qk.shape[0]

    def start(r):
        return jnp.clip(r - NA_WIN_H // 2, 0, rows_n - NA_WIN_H)

    E = pl.Element
    return pl.pallas_call(
        _na_kernel,
        grid=(rows_n,),
        in_specs=[
            pl.BlockSpec((GRID_W, naw), lambda r: (r, 0)),
            pl.BlockSpec((E(win), E(naw)), lambda r: (start(r) * GRID_W, naw)),
            pl.BlockSpec((E(win), E(naw)), lambda r: (start(r) * GRID_W, 0)),
            pl.BlockSpec((n_ctx, naw), lambda r: (0, 1)),
            pl.BlockSpec((n_ctx, naw), lambda r: (0, 0)),
            pl.BlockSpec((1, NA_HEADS, GRID_W, win + n_ctx), lambda r: (start(r) - r + NA_WIN_H - 1, 0, 0, 0)),
        ],
        out_specs=pl.BlockSpec((GRID_W, naw), lambda r: (r, 0)),
        out_shape=jax.ShapeDtypeStruct((m, naw), BF16),
        compiler_params=_params("parallel"),
        name="na_attention",
    )(na_qk, na_qk, na_v, ctx_qk, ctx_v, bias)


def _merge_kernel(h_ref, yn_ref, ym_ref, wgn_ref, wgm_ref, wbn_ref, wbm_ref, o_ref):
    h = h_ref[...]
    gn = jax.nn.sigmoid(jnp.dot(h, wgn_ref[...], preferred_element_type=F32))
    gm = jax.nn.sigmoid(jnp.dot(h, wgm_ref[...], preferred_element_type=F32))
    a = jnp.dot(yn_ref[...], wbn_ref[...], preferred_element_type=F32)
    b = jnp.dot(ym_ref[...], wbm_ref[...], preferred_element_type=F32)
    o_ref[...] = (gn * a + gm * b).astype(o_ref.dtype)


def _merge(h, y_na, y_mla, lw):
    m, d = h.shape
    bm = _tile(m, 512)
    bn = _tile(d, 512, LANES)
    nj = d // bn
    kn, km = y_na.shape[1], y_mla.shape[1]
    return pl.pallas_call(
        _merge_kernel,
        grid=(m // bm, nj),
        in_specs=[pl.BlockSpec((bm, d), lambda i, j: (i, 0)),
                  pl.BlockSpec((bm, kn), lambda i, j: (i, 0)),
                  pl.BlockSpec((bm, km), lambda i, j: (i, 0)),
                  _w_spec(lw["w_gates"], bn),
                  _w_spec(lw["w_gates"], bn, shift=nj),
                  _w_spec(lw["w_b_na"], bn),
                  _w_spec(lw["w_b_mla"], bn)],
        out_specs=pl.BlockSpec((bm, bn), lambda i, j: (i, j)),
        out_shape=jax.ShapeDtypeStruct((m, d), BF16),
        compiler_params=_params("parallel", "parallel"),
        name="branch_merge",
    )(h, y_na, y_mla, lw["w_gates"].arr, lw["w_gates"].arr, lw["w_b_na"].arr, lw["w_b_mla"].arr)


def _residual_matmul(a, w, x, gate, name):
    m, d = x.shape
    bm = _tile(m, 512)
    bn = _tile(d, 1024, LANES)
    return _matmul(a, w, _epi_residual, [x, gate.reshape(1, d)],
                   [pl.BlockSpec((bm, bn), lambda i, j: (i, j)),
                    pl.BlockSpec((1, bn), lambda i, j: (0, j))],
                   jax.ShapeDtypeStruct((m, d), F32),
                   pl.BlockSpec((bm, bn), lambda i, j: (i, j)), bm, bn, name)


def _ffn_up_kernel(x_ref, wg_ref, wv_ref, halo_ref, cw_ref, cb_ref, o_ref):
    x = x_ref[...]
    g = jnp.dot(x, wg_ref[...], preferred_element_type=F32)
    bm = g.shape[0]
    rows = lax.broadcasted_iota(jnp.int32, g.shape, 0)
    g_prev = jnp.where(rows == 0, halo_ref[0:1, :], pltpu.roll(g, 1, axis=0))
    g_next = jnp.where(rows == bm - 1, halo_ref[1:2, :], pltpu.roll(g, bm - 1, axis=0))
    y = g_prev * cw_ref[0:1, :] + g * cw_ref[1:2, :] + g_next * cw_ref[2:3, :] + cb_ref[...]
    v = jnp.dot(x, wv_ref[...], preferred_element_type=F32)
    o_ref[...] = (y * jax.nn.sigmoid(y) * v).astype(o_ref.dtype)


def _ffn_up(h2, lw):
    m, d = h2.shape
    dff = lw["conv_b"].shape[1]
    bm = _tile(m, 1024)
    bn = _tile(dff, 512, LANES)
    nt, nj = m // bm, dff // bn

    edge = h2.reshape(nt, bm, d)[:, (0, bm - 1), :].reshape(2 * nt, d)
    pad = (-edge.shape[0]) % 16
    edge = jnp.pad(edge, ((0, pad), (0, 0)))
    bne = _tile(dff, 1024, LANES)
    eg = _matmul(edge, lw["w_up"]._replace(n=dff), _epi_cast, [], [],
                 jax.ShapeDtypeStruct((edge.shape[0], dff), F32),
                 pl.BlockSpec((edge.shape[0], bne), lambda i, j: (i, j)),
                 edge.shape[0], bne, "ffn_edge_rows")
    eg = eg[:2 * nt].reshape(nt, 2, dff)
    zero = jnp.zeros((1, dff), F32)
    prev_rows = jnp.concatenate([zero, eg[:-1, 1]], axis=0)
    next_rows = jnp.concatenate([eg[1:, 0], zero], axis=0)
    halo = jnp.stack([prev_rows, next_rows], axis=1)

    return pl.pallas_call(
        _ffn_up_kernel,
        grid=(m // bm, nj),
        in_specs=[pl.BlockSpec((bm, d), lambda i, j: (i, 0)),
                  _w_spec(lw["w_up"], bn),
                  _w_spec(lw["w_up"], bn, shift=nj),
                  pl.BlockSpec((None, 2, bn), lambda i, j: (i, 0, j)),
                  pl.BlockSpec((CONV_W, bn), lambda i, j: (0, j)),
                  pl.BlockSpec((1, bn), lambda i, j: (0, j))],
        out_specs=pl.BlockSpec((bm, bn), lambda i, j: (i, j)),
        out_shape=jax.ShapeDtypeStruct((m, dff), BF16),
        compiler_params=_params("parallel", "parallel"),
        name="ffn_up_conv",
    )(h2, lw["w_up"].arr, lw["w_up"].arr, halo, lw["conv_w"], lw["conv_b"])


def _rope_partner():
    half = MLA_ROPE_DIM // 2
    quarter = half // 2
    idx = np.arange(MLA_ROPE_DIM)
    return np.where((idx % half) < quarter, idx + quarter, idx - quarter)


def _pad_lanes(a):
    return jnp.pad(a, [(0, 0)] * (a.ndim - 1) + [(0, LANES - a.shape[-1])])


def _stacked_weights(w_in, mla_w_q_up, mla_w_kv_up, w_branch_na, w_branch_mla, w_out, ffn_w_up, ffn_w_down):
    depth = w_in.shape[0]
    naw = NA_HEADS * NA_HEAD_DIM
    q_rank = mla_w_q_up.shape[1]
    kv_rank = mla_w_kv_up.shape[1]
    perm = _rope_partner()
    o_cq = 3 * naw
    o_ckv = o_cq + q_rank
    o_kr = o_ckv + kv_rank
    o_g = o_kr + MLA_ROPE_DIM
    w_in = w_in.astype(BF16)
    w_kr = w_in[:, :, o_kr:o_g]
    wq = mla_w_q_up.reshape(depth, q_rank, MLA_HEADS, MLA_QK_DIM)
    wq_rope = wq[..., MLA_NOPE_DIM:]
    return {
        "in_head": w_in,
        "ckv_kr": jnp.concatenate([w_in[:, :, o_ckv:o_kr], _pad_lanes(w_kr), _pad_lanes(w_kr[:, :, perm])], axis=2),
        "gates": w_in[:, :, o_g:],
        "q_up": jnp.concatenate([wq[..., :MLA_NOPE_DIM], _pad_lanes(wq_rope), _pad_lanes(wq_rope[..., perm])],
                                axis=3).reshape(depth, q_rank, MLA_HEADS * 3 * LANES).astype(BF16),
        "kv_up": mla_w_kv_up.astype(BF16),
        "b_na": w_branch_na.astype(BF16),
        "b_mla": w_branch_mla.astype(BF16),
        "out": w_out.astype(BF16),
        "up": ffn_w_up.astype(BF16),
        "down": ffn_w_down.astype(BF16),
    }


def _layer_weights(l, sw, na_q_norm, na_k_norm, mla_cq_norm, mla_ckv_norm, mla_q_norm, mla_k_norm,
                   ffn_conv_w, ffn_conv_b):
    naw = NA_HEADS * NA_HEAD_DIM
    q_rank = sw["q_up"].shape[1]
    kv_rank = sw["kv_up"].shape[1]
    perm = _rope_partner()
    whole = lambda a: _W(a, l, 0, a.shape[2])
    lw = {
        "w_na_qk": _W(sw["in_head"], l, 0, 2 * naw),
        "w_na_v": _W(sw["in_head"], l, 2 * naw, naw),
        "w_cq": _W(sw["in_head"], l, 3 * naw, q_rank),
        "w_ckv_kr": whole(sw["ckv_kr"]),
        "w_gates": whole(sw["gates"]),
        "w_q_up": whole(sw["q_up"]),
        "w_kv_up": whole(sw["kv_up"]),
        "w_b_na": whole(sw["b_na"]),
        "w_b_mla": whole(sw["b_mla"]),
        "w_out": whole(sw["out"]),
        "w_up": whole(sw["up"]),
        "w_down": whole(sw["down"]),
        "g_na_qk": jnp.concatenate([jnp.tile(na_q_norm[l] * (NA_HEAD_DIM ** -0.5 * LOG2E), NA_HEADS),
                                    jnp.tile(na_k_norm[l], NA_HEADS)]).reshape(1, 2 * naw),
        "g_cq": mla_cq_norm[l].reshape(1, q_rank),
        "g_ckv": mla_ckv_norm[l].reshape(1, kv_rank),
        "conv_w": ffn_conv_w[l],
        "conv_b": ffn_conv_b[l].reshape(1, -1),
    }
    for tag, g in (("q", mla_q_norm[l]), ("k", mla_k_norm[l])):
        g_rope = g[MLA_NOPE_DIM:]
        lw[f"g{tag}_nope"] = g[:MLA_NOPE_DIM].reshape(1, LANES)
        lw[f"g{tag}_rope"] = _pad_lanes(g_rope).reshape(1, LANES)
        lw[f"g{tag}_swap"] = _pad_lanes(g_rope[perm]).reshape(1, LANES)
    return lw


def _rope_tables(n_tok):
    t = jnp.arange(n_tok, dtype=jnp.int32)
    half = MLA_ROPE_DIM // 2
    inv_freq = ROPE_THETA ** (-jnp.arange(0, half, 2, dtype=F32) / half)
    cs, sn = [], []
    for pos in (t // GRID_W, t % GRID_W):
        ang = pos.astype(F32)[:, None] * inv_freq[None, :]
        cs += [jnp.cos(ang), jnp.cos(ang)]
        sn += [-jnp.sin(ang), jnp.sin(ang)]
    return _pad_lanes(jnp.concatenate(cs, axis=1)), _pad_lanes(jnp.concatenate(sn, axis=1))


def _identity_tables(n_tok):
    cos_t = _pad_lanes(jnp.ones((n_tok, MLA_ROPE_DIM), F32))
    return cos_t, jnp.zeros((n_tok, LANES), F32)


def _head_major(a, heads):
    rows = a.shape[0]
    return a.reshape(rows, heads, -1).transpose(1, 0, 2)


def _head_major_vt(v, heads):
    rows = v.shape[0]
    vt = v.reshape(rows, heads, MLA_V_DIM).transpose(1, 2, 0)
    tail = jnp.zeros((heads, BF16_SUBLANES, rows), v.dtype).at[:, 0, :].set(1.0)
    return jnp.concatenate([vt, tail], axis=1)


def kernel(x, c, ctx, c_ctx, ada_down, ada_up, ada_bias, norm_mix, norm_ffn, w_in, na_q_norm, na_k_norm, na_rpb, mla_cq_norm, mla_ckv_norm, mla_w_q_up, mla_w_kv_up, mla_q_norm, mla_k_norm, w_branch_na, w_branch_mla, w_out, ffn_w_up, ffn_conv_w, ffn_conv_b, ffn_w_down):
    batch, n_tok, d = x.shape
    assert batch == 1 and c.shape[0] == 1
    depth = w_in.shape[0]
    naw = NA_HEADS * NA_HEAD_DIM
    xs = x[0]
    cs = ctx[0]
    lat_tabs = _rope_tables(n_tok)
    ctx_tabs = _identity_tables(cs.shape[0])
    cond = jnp.zeros((8, d), F32).at[0].set(c[0]).at[1].set(c_ctx)
    sw = _stacked_weights(w_in, mla_w_q_up, mla_w_kv_up, w_branch_na, w_branch_mla, w_out, ffn_w_up, ffn_w_down)

    for l in range(depth):
        last = l == depth - 1
        lw = _layer_weights(l, sw, na_q_norm, na_k_norm, mla_cq_norm, mla_ckv_norm, mla_q_norm, mla_k_norm,
                            ffn_conv_w, ffn_conv_b)
        mods = _adaln(cond, ada_down, ada_up, ada_bias, l)
        sh1, sc1, g1, sh2, sc2, g2 = [mods[0, i * d:(i + 1) * d] for i in range(N_MOD)]
        csh1, csc1, cg1, csh2, csc2, cg2 = [mods[1, i * d:(i + 1) * d] for i in range(N_MOD)]

        h = _norm_mod(xs, norm_mix[l], sc1, sh1)
        hc = _norm_mod(cs, norm_mix[l], csc1, csh1)
        na_qk, na_v, mq, mk, mv = _project(h, lw, lat_tabs)
        c_qk, c_v, cmq, cmk, cmv = _project(hc, lw, ctx_tabs)

        y_na = _na_attention(na_qk, na_v, c_qk, c_v, _na_bias_table(na_rpb[l], cs.shape[0]))
        y_mla = _flash(mq, mk, mv, (cmk, cmv), name="mla_attention")
        u = _merge(h, y_na, y_mla, lw)
        xs = _residual_matmul(u, lw["w_out"], xs, g1, "out_proj")
        h2 = _norm_mod(xs, norm_ffn[l], sc2, sh2)
        xs = _residual_matmul(_ffn_up(h2, lw), lw["w_down"], xs, g2, "ffn_down")

        if not last:
            yc_na = _flash(_head_major(c_qk[:, :naw], NA_HEADS), _head_major(c_qk[:, naw:], NA_HEADS),
                           _head_major_vt(c_v, NA_HEADS), None, name="ctx_na_attention")
            yc_mla = _flash(cmq, cmk, cmv, None, name="ctx_mla_attention")
            uc = _merge(hc, yc_na, yc_mla, lw)
            cs = _residual_matmul(uc, lw["w_out"], cs, cg1, "ctx_out_proj")
            hc2 = _norm_mod(cs, norm_ffn[l], csc2, csh2)
            cs = _residual_matmul(_ffn_up(hc2, lw), lw["w_down"], cs, cg2, "ctx_ffn_down")
    return xs[None]
```

```python
import functools
from typing import NamedTuple

import numpy as np
import jax
import jax.numpy as jnp
from jax import lax
from jax.experimental import pallas as pl
from jax.experimental.pallas import tpu as pltpu

F32 = jnp.float32
BF16 = jnp.bfloat16

GRID_W = 64
EPS = 1e-6
NEG_INF = -1e30
N_MOD = 6

NA_HEADS = 16
NA_HEAD_DIM = 128
NA_WIN_H = 8
NA_WIN_W = 16
NA_GROUP = 2

MLA_HEADS = 16
MLA_NOPE_DIM = 128
MLA_ROPE_DIM = 64
MLA_QK_DIM = MLA_NOPE_DIM + MLA_ROPE_DIM
MLA_V_DIM = 128
ROPE_THETA = 10000.0
CONV_W = 3

LANES = 128
MLA_QK_PAD = 2 * LANES
BF16_SUBLANES = 16
VT_ROWS = MLA_V_DIM + BF16_SUBLANES
LOG2E = 1.4426950408889634
VMEM_LIMIT_BYTES = 56 * 1024 * 1024


def _tile(dim, pref, mult=8):
    if dim <= pref:
        return dim
    t = (pref // mult) * mult
    while t >= mult:
        if dim % t == 0:
            return t
        t -= mult
    return dim


def _params(*sem):
    return pltpu.CompilerParams(dimension_semantics=sem, vmem_limit_bytes=VMEM_LIMIT_BYTES)


def _ada_down_kernel(c_ref, w_ref, o_ref):
    c = c_ref[...]
    a = (c * jax.nn.sigmoid(c)).astype(BF16)
    o_ref[...] = jnp.dot(a, w_ref[...].astype(BF16), preferred_element_type=F32)


def _ada_up_kernel(t_ref, w_ref, b_ref, o_ref):
    o_ref[...] = jnp.dot(t_ref[...].astype(BF16), w_ref[...].astype(BF16),
                         preferred_element_type=F32) + b_ref[...]


def _adaln(cond, down, up, bias, layer):
    rows, d = cond.shape
    rank = down.shape[2]
    n = up.shape[2]
    bn1 = _tile(rank, 512, LANES)
    t = pl.pallas_call(
        _ada_down_kernel,
        grid=(rank // bn1,),
        in_specs=[pl.BlockSpec((rows, d), lambda j: (0, 0)),
                  pl.BlockSpec((None, d, bn1), lambda j: (layer, 0, j))],
        out_specs=pl.BlockSpec((rows, bn1), lambda j: (0, j)),
        out_shape=jax.ShapeDtypeStruct((rows, rank), F32),
        compiler_params=_params("parallel"),
        name="ada_down",
    )(cond, down)
    bn2 = _tile(n, 2048, LANES)
    return pl.pallas_call(
        _ada_up_kernel,
        grid=(n // bn2,),
        in_specs=[pl.BlockSpec((rows, rank), lambda j: (0, 0)),
                  pl.BlockSpec((None, rank, bn2), lambda j: (layer, 0, j)),
                  pl.BlockSpec((None, 1, bn2), lambda j: (layer, 0, j))],
        out_specs=pl.BlockSpec((rows, bn2), lambda j: (0, j)),
        out_shape=jax.ShapeDtypeStruct((rows, n), F32),
        compiler_params=_params("parallel"),
        name="ada_up",
    )(t, up, bias.reshape(bias.shape[0], 1, n))


def _norm_mod_kernel(x_ref, g_ref, sc_ref, sh_ref, o_ref):
    x = x_ref[...]
    y = x * lax.rsqrt(jnp.mean(x * x, axis=-1, keepdims=True) + EPS) * g_ref[...]
    o_ref[...] = (y * (1.0 + sc_ref[...]) + sh_ref[...]).astype(o_ref.dtype)


def _norm_mod(x, g, sc, sh):
    m, d = x.shape
    bm = _tile(m, 512)
    vec = pl.BlockSpec((1, d), lambda i: (0, 0))
    return pl.pallas_call(
        _norm_mod_kernel,
        grid=(m // bm,),
        in_specs=[pl.BlockSpec((bm, d), lambda i: (i, 0)), vec, vec, vec],
        out_specs=pl.BlockSpec((bm, d), lambda i: (i, 0)),
        out_shape=jax.ShapeDtypeStruct((m, d), BF16),
        compiler_params=_params("parallel"),
        name="norm_mod",
    )(x, g.reshape(1, d), sc.reshape(1, d), sh.reshape(1, d))


def _mm_kernel(x_ref, w_ref, *rest, epilogue):
    acc = jnp.dot(x_ref[...], w_ref[...], preferred_element_type=F32)
    epilogue(acc, *rest)


class _W(NamedTuple):
    arr: jax.Array
    layer: int
    col0: int
    n: int


def _w_spec(w, bn, shift=0):
    blk0, rem = divmod(w.col0, bn)
    assert rem == 0 and w.n % bn == 0
    return pl.BlockSpec((None, w.arr.shape[1], bn), lambda i, j: (w.layer, 0, blk0 + shift + j))


def _matmul(x, w, epilogue, extras, extra_specs, out_shapes, out_specs, bm, bn, name):
    m, k = x.shape
    return pl.pallas_call(
        functools.partial(_mm_kernel, epilogue=epilogue),
        grid=(m // bm, w.n // bn),
        in_specs=[pl.BlockSpec((bm, k), lambda i, j: (i, 0)), _w_spec(w, bn)] + list(extra_specs),
        out_specs=out_specs,
        out_shape=out_shapes,
        compiler_params=_params("parallel", "parallel"),
        name=name,
    )(x, w.arr, *extras)


def _epi_cast(acc, o_ref):
    o_ref[...] = acc.astype(o_ref.dtype)


def _epi_headnorm(acc, g_ref, o_ref):
    for c in range(acc.shape[1] // LANES):
        sl = slice(c * LANES, (c + 1) * LANES)
        a = acc[:, sl]
        r = lax.rsqrt(jnp.mean(a * a, axis=-1, keepdims=True) + EPS)
        o_ref[:, sl] = (a * r * g_ref[:, sl]).astype(o_ref.dtype)


def _epi_rownorm(acc, g_ref, o_ref):
    r = lax.rsqrt(jnp.mean(acc * acc, axis=-1, keepdims=True) + EPS)
    o_ref[...] = (acc * r * g_ref[...]).astype(o_ref.dtype)


def _epi_ckv(acc, g_ref, ckv_ref, kr_ref, *, kv_rank):
    a = acc[:, :kv_rank]
    r = lax.rsqrt(jnp.mean(a * a, axis=-1, keepdims=True) + EPS)
    ckv_ref[...] = (a * r * g_ref[...]).astype(ckv_ref.dtype)
    kr_ref[...] = acc[:, kv_rank:]


def _epi_q_up(acc, c_ref, s_ref, gn_ref, gr_ref, gs_ref, o_ref, *, heads, scale):
    cg = c_ref[...] * gr_ref[...]
    sg = s_ref[...] * gs_ref[...]
    for h in range(heads):
        base = h * 3 * LANES
        nope = acc[:, base:base + LANES]
        rp = acc[:, base + LANES:base + 2 * LANES]
        sw = acc[:, base + 2 * LANES:base + 3 * LANES]
        ss = jnp.sum(nope * nope, axis=-1, keepdims=True) + jnp.sum(rp * rp, axis=-1, keepdims=True)
        r = lax.rsqrt(ss * (1.0 / MLA_QK_DIM) + EPS) * scale
        o_ref[h, :, 0:LANES] = (nope * r * gn_ref[...]).astype(o_ref.dtype)
        o_ref[h, :, LANES:2 * LANES] = ((rp * cg + sw * sg) * r).astype(o_ref.dtype)


def _ones_row_tile(cols, dtype):
    rows = lax.broadcasted_iota(jnp.int32, (BF16_SUBLANES, cols), 0)
    return jnp.where(rows == 0, 1.0, 0.0).astype(dtype)


def _epi_kv_up(acc, kr_ref, c_ref, s_ref, gn_ref, gr_ref, gs_ref, k_ref, vt_ref, *, heads):
    kr = kr_ref[:, 0:LANES]
    krs = kr_ref[:, LANES:2 * LANES]
    ssr = jnp.sum(kr * kr, axis=-1, keepdims=True)
    rope = kr * (c_ref[...] * gr_ref[...]) + krs * (s_ref[...] * gs_ref[...])
    for h in range(heads):
        base = h * 2 * LANES
        nope = acc[:, base:base + LANES]
        ss = jnp.sum(nope * nope, axis=-1, keepdims=True) + ssr
        r = lax.rsqrt(ss * (1.0 / MLA_QK_DIM) + EPS)
        k_ref[h, :, 0:LANES] = (nope * r * gn_ref[...]).astype(k_ref.dtype)
        k_ref[h, :, LANES:2 * LANES] = (rope * r).astype(k_ref.dtype)
        vt_ref[h, 0:MLA_V_DIM, :] = acc[:, base + LANES:base + 2 * LANES].T.astype(vt_ref.dtype)
        vt_ref[h, MLA_V_DIM:VT_ROWS, :] = _ones_row_tile(acc.shape[0], vt_ref.dtype)


def _epi_residual(acc, x_ref, g_ref, o_ref):
    o_ref[...] = x_ref[...] + g_ref[...] * acc


def _project(h, lw, tabs):
    m, d = h.shape
    bm = _tile(m, 1024)
    naw = NA_HEADS * NA_HEAD_DIM
    row = lambda n: pl.BlockSpec((1, n), lambda i, j: (0, 0))

    bn = _tile(2 * naw, 1024, LANES)
    na_qk = _matmul(h, lw["w_na_qk"], _epi_headnorm, [lw["g_na_qk"]],
                    [pl.BlockSpec((1, bn), lambda i, j: (0, j))],
                    jax.ShapeDtypeStruct((m, 2 * naw), BF16),
                    pl.BlockSpec((bm, bn), lambda i, j: (i, j)), bm, bn, "na_qk_proj")
    bn = _tile(naw, 1024, LANES)
    na_v = _matmul(h, lw["w_na_v"], _epi_cast, [], [],
                   jax.ShapeDtypeStruct((m, naw), BF16),
                   pl.BlockSpec((bm, bn), lambda i, j: (i, j)), bm, bn, "na_v_proj")

    q_rank = lw["w_cq"].n
    kv_rank = lw["w_ckv_kr"].n - 2 * LANES
    bmc = _tile(m, 512)
    cqn = _matmul(h, lw["w_cq"], _epi_rownorm, [lw["g_cq"]], [row(q_rank)],
                  jax.ShapeDtypeStruct((m, q_rank), BF16),
                  pl.BlockSpec((bmc, q_rank), lambda i, j: (i, 0)), bmc, q_rank, "cq_proj")
    ckvn, kr = _matmul(h, lw["w_ckv_kr"], functools.partial(_epi_ckv, kv_rank=kv_rank),
                       [lw["g_ckv"]], [row(kv_rank)],
                       (jax.ShapeDtypeStruct((m, kv_rank), BF16),
                        jax.ShapeDtypeStruct((m, 2 * LANES), F32)),
                       (pl.BlockSpec((bmc, kv_rank), lambda i, j: (i, 0)),
                        pl.BlockSpec((bmc, 2 * LANES), lambda i, j: (i, 0))),
                       bmc, kv_rank + 2 * LANES, "ckv_proj")

    cos_t, sin_t = tabs
    hb = 4 if MLA_HEADS % 4 == 0 else 1
    bmq = _tile(m, 512)
    tab = pl.BlockSpec((bmq, LANES), lambda i, j: (i, 0))
    vec = pl.BlockSpec((1, LANES), lambda i, j: (0, 0))
    mla_q = _matmul(cqn, lw["w_q_up"],
                    functools.partial(_epi_q_up, heads=hb, scale=MLA_QK_DIM ** -0.5 * LOG2E),
                    [cos_t, sin_t, lw["gq_nope"], lw["gq_rope"], lw["gq_swap"]],
                    [tab, tab, vec, vec, vec],
                    jax.ShapeDtypeStruct((MLA_HEADS, m, MLA_QK_PAD), BF16),
                    pl.BlockSpec((hb, bmq, MLA_QK_PAD), lambda i, j: (j, i, 0)),
                    bmq, hb * 3 * LANES, "mla_q_up")
    bmk = _tile(m, 1024)
    tab = pl.BlockSpec((bmk, LANES), lambda i, j: (i, 0))
    mla_k, mla_v = _matmul(ckvn, lw["w_kv_up"], functools.partial(_epi_kv_up, heads=hb),
                           [kr, cos_t, sin_t, lw["gk_nope"], lw["gk_rope"], lw["gk_swap"]],
                           [pl.BlockSpec((bmk, 2 * LANES), lambda i, j: (i, 0)), tab, tab, vec, vec, vec],
                           (jax.ShapeDtypeStruct((MLA_HEADS, m, MLA_QK_PAD), BF16),
                            jax.ShapeDtypeStruct((MLA_HEADS, VT_ROWS, m), BF16)),
                           (pl.BlockSpec((hb, bmk, MLA_QK_PAD), lambda i, j: (j, i, 0)),
                            pl.BlockSpec((hb, VT_ROWS, bmk), lambda i, j: (j, 0, i))),
                           bmk, hb * 2 * LANES, "mla_kv_up")
    return na_qk, na_v, mla_q, mla_k, mla_v


def _dot_nt(a, b):
    return lax.dot_general(a, b, (((1,), (1,)), ((), ())), preferred_element_type=F32)


def _softmax_pv(s_t, vt_blk, state):
    blk_max = jnp.max(s_t, axis=0, keepdims=True)
    m_new = blk_max if state is None else jnp.maximum(state[0], blk_max)
    p_t = jnp.exp2(s_t - m_new).astype(BF16)
    upd = jnp.dot(vt_blk, p_t, preferred_element_type=F32)
    if state is None:
        return m_new, upd
    return m_new, jnp.exp2(state[0] - m_new) * state[1] + upd


def _flash_kernel(*refs, tq, tk, n_chunks, has_ctx):
    if has_ctx:
        q_ref, k_ref, vt_ref, kc_ref, vct_ref, o_ref, s_scr = refs
    else:
        q_ref, k_ref, vt_ref, o_ref, s_scr = refs
    n_sub = q_ref.shape[0] // tq
    sub_rows = [slice(i * tq, (i + 1) * tq) for i in range(n_sub)]
    qs = [q_ref[r, :] for r in sub_rows]

    def chunk(c):
        return pl.ds(pl.multiple_of(c * tk, tk), tk)

    def scores(c, slot):
        k_blk = k_ref[chunk(c), :]
        for i, q in enumerate(qs):
            s_scr[slot, i, :, 0:tq] = _dot_nt(k_blk, q)

    def consume(c, slot, state):
        vt_blk = vt_ref[:, chunk(c)]
        return tuple(_softmax_pv(s_scr[slot, i, :, 0:tq], vt_blk, st) for i, st in enumerate(state))

    first = 0 if has_ctx else 1
    n_pairs = (n_chunks - first) // 2
    if has_ctx:
        init_k, init_vt = kc_ref[...], vct_ref[...]
    else:
        init_k, init_vt = k_ref[0:tk, :], vt_ref[:, 0:tk]
    init_scores = [_dot_nt(init_k, q) for q in qs]
    if n_pairs > 0:
        scores(first, 0)
    state = tuple(_softmax_pv(s, init_vt, None) for s in init_scores)

    if n_pairs > 0:

        def pair(j, state, prefetch):
            c0 = first + 2 * j
            scores(c0 + 1, 1)
            state = consume(c0, 0, state)
            if prefetch:
                scores(c0 + 2, 0)
            return consume(c0 + 1, 1, state)

        state = lax.fori_loop(0, n_pairs - 1, lambda j, st: pair(j, st, True), state)
        state = pair(n_pairs - 1, state, False)
    if (n_chunks - first) % 2:
        last = slice((n_chunks - 1) * tk, n_chunks * tk)
        state = tuple(_softmax_pv(_dot_nt(k_ref[last, :], q), vt_ref[:, last], st) for q, st in zip(qs, state))

    for rows, (_, acc) in zip(sub_rows, state):
        o_t = acc[0:MLA_V_DIM, :] / acc[MLA_V_DIM:MLA_V_DIM + 1, :]
        o_ref[rows, :] = o_t.T.astype(o_ref.dtype)


def _flash(q, k, vt, ctx_kv, *, name):
    heads, mq, dq = q.shape
    mk = k.shape[1]
    bq = _tile(mq, 512)
    tq = _tile(bq, 512)
    tk = _tile(mk, 2048, LANES)
    q_spec = pl.BlockSpec((None, bq, dq), lambda h, i: (h, i, 0))
    whole = lambda a: pl.BlockSpec((None,) + a.shape[1:], lambda h, i: (h, 0, 0))
    in_specs = [q_spec, whole(k), whole(vt)]
    args = [q, k, vt]
    if ctx_kv is not None:
        in_specs += [whole(ctx_kv[0]), whole(ctx_kv[1])]
        args += list(ctx_kv)
    return pl.pallas_call(
        functools.partial(_flash_kernel, tq=tq, tk=tk, n_chunks=mk // tk, has_ctx=ctx_kv is not None),
        grid=(heads, mq // bq),
        in_specs=in_specs,
        out_specs=pl.BlockSpec((bq, MLA_V_DIM), lambda h, i: (i, h)),
        out_shape=jax.ShapeDtypeStruct((mq, heads * MLA_V_DIM), BF16),
        scratch_shapes=[pltpu.VMEM((2, bq // tq, tk, tq + LANES), F32)],
        compiler_params=_params("parallel", "parallel"),
        name=name,
    )(*args)


def _na_kernel(q_ref, k_ref, v_ref, kc_ref, vc_ref, b_ref, o_ref):
    head = lambda h: slice(h * NA_HEAD_DIM, (h + 1) * NA_HEAD_DIM)

    def group_scores(g):
        return [_dot_nt(q_ref[:, head(h)], jnp.concatenate([k_ref[:, head(h)], kc_ref[:, head(h)]], axis=0))
                + b_ref[0, h] for h in range(g * NA_GROUP, (g + 1) * NA_GROUP)]

    n_groups = NA_HEADS // NA_GROUP
    outs = []
    nxt = group_scores(0)
    for g in range(n_groups):
        cur = nxt
        if g + 1 < n_groups:
            nxt = group_scores(g + 1)
        probs, inv_l = [], []
        for s in cur:
            p = jnp.exp2(s - jnp.max(s, axis=-1, keepdims=True))
            inv_l.append(1.0 / jnp.sum(p, axis=-1, keepdims=True))
            probs.append(p.astype(BF16))
        for i, h in enumerate(range(g * NA_GROUP, (g + 1) * NA_GROUP)):
            v_all = jnp.concatenate([v_ref[:, head(h)], vc_ref[:, head(h)]], axis=0)
            outs.append(jnp.dot(probs[i], v_all, preferred_element_type=F32) * inv_l[i])
    o_ref[...] = jnp.concatenate(outs, axis=-1).astype(o_ref.dtype)


def _na_bias_table(rpb, n_ctx):
    qc = np.arange(GRID_W)
    kc = np.arange(GRID_W)
    cs = np.clip(qc - NA_WIN_W // 2, 0, GRID_W - NA_WIN_W)
    col_ok = (kc[None, :] >= cs[:, None]) & (kc[None, :] < cs[:, None] + NA_WIN_W)
    col_idx = np.clip(kc[None, :] - qc[:, None] + NA_WIN_W - 1, 0, 2 * NA_WIN_W - 2)
    cols = rpb.astype(F32)[:, :, col_idx]
    cols = jnp.where(jnp.asarray(col_ok)[None, None], cols * LOG2E, NEG_INF)
    variants = [cols[:, v:v + NA_WIN_H].transpose(0, 2, 1, 3).reshape(NA_HEADS, GRID_W, NA_WIN_H * GRID_W)
                for v in range(NA_WIN_H)]
    table = jnp.stack(variants)
    return jnp.concatenate([table, jnp.zeros(table.shape[:3] + (n_ctx,), F32)], axis=-1)


def _na_attention(na_qk, na_v, ctx_qk, ctx_v, bias):
    m = na_qk.shape[0]
    naw = NA_HEADS * NA_HEAD_DIM
    rows_n = m // GRID_W
    win = NA_WIN_H * GRID_W
    n_ctx = ctx_qk.shape[0]

    def start(r):
        return jnp.clip(r - NA_WIN_H // 2, 0, rows_n - NA_WIN_H)

    E = pl.Element
    return pl.pallas_call(
        _na_kernel,
        grid=(rows_n,),
        in_specs=[
            pl.BlockSpec((GRID_W, naw), lambda r: (r, 0)),
            pl.BlockSpec((E(win), E(naw)), lambda r: (start(r) * GRID_W, naw)),
            pl.BlockSpec((E(win), E(naw)), lambda r: (start(r) * GRID_W, 0)),
            pl.BlockSpec((n_ctx, naw), lambda r: (0, 1)),
            pl.BlockSpec((n_ctx, naw), lambda r: (0, 0)),
            pl.BlockSpec((1, NA_HEADS, GRID_W, win + n_ctx), lambda r: (start(r) - r + NA_WIN_H - 1, 0, 0, 0)),
        ],
        out_specs=pl.BlockSpec((GRID_W, naw), lambda r: (r, 0)),
        out_shape=jax.ShapeDtypeStruct((m, naw), BF16),
        compiler_params=_params("parallel"),
        name="na_attention",
    )(na_qk, na_qk, na_v, ctx_qk, ctx_v, bias)


def _merge_kernel(h_ref, yn_ref, ym_ref, wgn_ref, wgm_ref, wbn_ref, wbm_ref, o_ref):
    h = h_ref[...]
    gn = jax.nn.sigmoid(jnp.dot(h, wgn_ref[...], preferred_element_type=F32))
    gm = jax.nn.sigmoid(jnp.dot(h, wgm_ref[...], preferred_element_type=F32))
    a = jnp.dot(yn_ref[...], wbn_ref[...], preferred_element_type=F32)
    b = jnp.dot(ym_ref[...], wbm_ref[...], preferred_element_type=F32)
    o_ref[...] = (gn * a + gm * b).astype(o_ref.dtype)


def _merge(h, y_na, y_mla, lw):
    m, d = h.shape
    bm = _tile(m, 512)
    bn = _tile(d, 512, LANES)
    nj = d // bn
    kn, km = y_na.shape[1], y_mla.shape[1]
    return pl.pallas_call(
        _merge_kernel,
        grid=(m // bm, nj),
        in_specs=[pl.BlockSpec((bm, d), lambda i, j: (i, 0)),
                  pl.BlockSpec((bm, kn), lambda i, j: (i, 0)),
                  pl.BlockSpec((bm, km), lambda i, j: (i, 0)),
                  _w_spec(lw["w_gates"], bn),
                  _w_spec(lw["w_gates"], bn, shift=nj),
                  _w_spec(lw["w_b_na"], bn),
                  _w_spec(lw["w_b_mla"], bn)],
        out_specs=pl.BlockSpec((bm, bn), lambda i, j: (i, j)),
        out_shape=jax.ShapeDtypeStruct((m, d), BF16),
        compiler_params=_params("parallel", "parallel"),
        name="branch_merge",
    )(h, y_na, y_mla, lw["w_gates"].arr, lw["w_gates"].arr, lw["w_b_na"].arr, lw["w_b_mla"].arr)


def _residual_matmul(a, w, x, gate, name):
    m, d = x.shape
    bm = _tile(m, 1024)
    bn = _tile(d, 512, LANES)
    return _matmul(a, w, _epi_residual, [x, gate.reshape(1, d)],
                   [pl.BlockSpec((bm, bn), lambda i, j: (i, j)),
                    pl.BlockSpec((1, bn), lambda i, j: (0, j))],
                   jax.ShapeDtypeStruct((m, d), F32),
                   pl.BlockSpec((bm, bn), lambda i, j: (i, j)), bm, bn, name)


def _ffn_up_kernel(x_ref, wg_ref, wv_ref, halo_ref, cw_ref, cb_ref, o_ref):
    x = x_ref[...]
    g = jnp.dot(x, wg_ref[...], preferred_element_type=F32)
    bm = g.shape[0]
    rows = lax.broadcasted_iota(jnp.int32, g.shape, 0)
    g_prev = jnp.where(rows == 0, halo_ref[0:1, :], pltpu.roll(g, 1, axis=0))
    g_next = jnp.where(rows == bm - 1, halo_ref[1:2, :], pltpu.roll(g, bm - 1, axis=0))
    y = g_prev * cw_ref[0:1, :] + g * cw_ref[1:2, :] + g_next * cw_ref[2:3, :] + cb_ref[...]
    v = jnp.dot(x, wv_ref[...], preferred_element_type=F32)
    o_ref[...] = (y * jax.nn.sigmoid(y) * v).astype(o_ref.dtype)


def _ffn_up(h2, lw):
    m, d = h2.shape
    dff = lw["conv_b"].shape[1]
    bm = _tile(m, 1024)
    bn = _tile(dff, 512, LANES)
    nt, nj = m // bm, dff // bn

    edge = h2.reshape(nt, bm, d)[:, (0, bm - 1), :].reshape(2 * nt, d)
    pad = (-edge.shape[0]) % 16
    edge = jnp.pad(edge, ((0, pad), (0, 0)))
    bne = _tile(dff, 1024, LANES)
    eg = _matmul(edge, lw["w_up"]._replace(n=dff), _epi_cast, [], [],
                 jax.ShapeDtypeStruct((edge.shape[0], dff), F32),
                 pl.BlockSpec((edge.shape[0], bne), lambda i, j: (i, j)),
                 edge.shape[0], bne, "ffn_edge_rows")
    eg = eg[:2 * nt].reshape(nt, 2, dff)
    zero = jnp.zeros((1, dff), F32)
    prev_rows = jnp.concatenate([zero, eg[:-1, 1]], axis=0)
    next_rows = jnp.concatenate([eg[1:, 0], zero], axis=0)
    halo = jnp.stack([prev_rows, next_rows], axis=1)

    return pl.pallas_call(
        _ffn_up_kernel,
        grid=(m // bm, nj),
        in_specs=[pl.BlockSpec((bm, d), lambda i, j: (i, 0)),
                  _w_spec(lw["w_up"], bn),
                  _w_spec(lw["w_up"], bn, shift=nj),
                  pl.BlockSpec((None, 2, bn), lambda i, j: (i, 0, j)),
                  pl.BlockSpec((CONV_W, bn), lambda i, j: (0, j)),
                  pl.BlockSpec((1, bn), lambda i, j: (0, j))],
        out_specs=pl.BlockSpec((bm, bn), lambda i, j: (i, j)),
        out_shape=jax.ShapeDtypeStruct((m, dff), BF16),
        compiler_params=_params("parallel", "parallel"),
        name="ffn_up_conv",
    )(h2, lw["w_up"].arr, lw["w_up"].arr, halo, lw["conv_w"], lw["conv_b"])


def _rope_partner():
    half = MLA_ROPE_DIM // 2
    quarter = half // 2
    idx = np.arange(MLA_ROPE_DIM)
    return np.where((idx % half) < quarter, idx + quarter, idx - quarter)


def _pad_lanes(a):
    return jnp.pad(a, [(0, 0)] * (a.ndim - 1) + [(0, LANES - a.shape[-1])])


def _stacked_weights(w_in, mla_w_q_up, mla_w_kv_up, w_branch_na, w_branch_mla, w_out, ffn_w_up, ffn_w_down):
    depth = w_in.shape[0]
    naw = NA_HEADS * NA_HEAD_DIM
    q_rank = mla_w_q_up.shape[1]
    kv_rank = mla_w_kv_up.shape[1]
    perm = _rope_partner()
    o_cq = 3 * naw
    o_ckv = o_cq + q_rank
    o_kr = o_ckv + kv_rank
    o_g = o_kr + MLA_ROPE_DIM
    w_in = w_in.astype(BF16)
    w_kr = w_in[:, :, o_kr:o_g]
    wq = mla_w_q_up.reshape(depth, q_rank, MLA_HEADS, MLA_QK_DIM)
    wq_rope = wq[..., MLA_NOPE_DIM:]
    return {
        "in_head": w_in,
        "ckv_kr": jnp.concatenate([w_in[:, :, o_ckv:o_kr], _pad_lanes(w_kr), _pad_lanes(w_kr[:, :, perm])], axis=2),
        "gates": w_in[:, :, o_g:],
        "q_up": jnp.concatenate([wq[..., :MLA_NOPE_DIM], _pad_lanes(wq_rope), _pad_lanes(wq_rope[..., perm])],
                                axis=3).reshape(depth, q_rank, MLA_HEADS * 3 * LANES).astype(BF16),
        "kv_up": mla_w_kv_up.astype(BF16),
        "b_na": w_branch_na.astype(BF16),
        "b_mla": w_branch_mla.astype(BF16),
        "out": w_out.astype(BF16),
        "up": ffn_w_up.astype(BF16),
        "down": ffn_w_down.astype(BF16),
    }


def _layer_weights(l, sw, na_q_norm, na_k_norm, mla_cq_norm, mla_ckv_norm, mla_q_norm, mla_k_norm,
                   ffn_conv_w, ffn_conv_b):
    naw = NA_HEADS * NA_HEAD_DIM
    q_rank = sw["q_up"].shape[1]
    kv_rank = sw["kv_up"].shape[1]
    perm = _rope_partner()
    whole = lambda a: _W(a, l, 0, a.shape[2])
    lw = {
        "w_na_qk": _W(sw["in_head"], l, 0, 2 * naw),
        "w_na_v": _W(sw["in_head"], l, 2 * naw, naw),
        "w_cq": _W(sw["in_head"], l, 3 * naw, q_rank),
        "w_ckv_kr": whole(sw["ckv_kr"]),
        "w_gates": whole(sw["gates"]),
        "w_q_up": whole(sw["q_up"]),
        "w_kv_up": whole(sw["kv_up"]),
        "w_b_na": whole(sw["b_na"]),
        "w_b_mla": whole(sw["b_mla"]),
        "w_out": whole(sw["out"]),
        "w_up": whole(sw["up"]),
        "w_down": whole(sw["down"]),
        "g_na_qk": jnp.concatenate([jnp.tile(na_q_norm[l] * (NA_HEAD_DIM ** -0.5 * LOG2E), NA_HEADS),
                                    jnp.tile(na_k_norm[l], NA_HEADS)]).reshape(1, 2 * naw),
        "g_cq": mla_cq_norm[l].reshape(1, q_rank),
        "g_ckv": mla_ckv_norm[l].reshape(1, kv_rank),
        "conv_w": ffn_conv_w[l],
        "conv_b": ffn_conv_b[l].reshape(1, -1),
    }
    for tag, g in (("q", mla_q_norm[l]), ("k", mla_k_norm[l])):
        g_rope = g[MLA_NOPE_DIM:]
        lw[f"g{tag}_nope"] = g[:MLA_NOPE_DIM].reshape(1, LANES)
        lw[f"g{tag}_rope"] = _pad_lanes(g_rope).reshape(1, LANES)
        lw[f"g{tag}_swap"] = _pad_lanes(g_rope[perm]).reshape(1, LANES)
    return lw


def _rope_tables(n_tok):
    t = jnp.arange(n_tok, dtype=jnp.int32)
    half = MLA_ROPE_DIM // 2
    inv_freq = ROPE_THETA ** (-jnp.arange(0, half, 2, dtype=F32) / half)
    cs, sn = [], []
    for pos in (t // GRID_W, t % GRID_W):
        ang = pos.astype(F32)[:, None] * inv_freq[None, :]
        cs += [jnp.cos(ang), jnp.cos(ang)]
        sn += [-jnp.sin(ang), jnp.sin(ang)]
    return _pad_lanes(jnp.concatenate(cs, axis=1)), _pad_lanes(jnp.concatenate(sn, axis=1))


def _identity_tables(n_tok):
    cos_t = _pad_lanes(jnp.ones((n_tok, MLA_ROPE_DIM), F32))
    return cos_t, jnp.zeros((n_tok, LANES), F32)


def _head_major(a, heads):
    rows = a.shape[0]
    return a.reshape(rows, heads, -1).transpose(1, 0, 2)


def _head_major_vt(v, heads):
    rows = v.shape[0]
    vt = v.reshape(rows, heads, MLA_V_DIM).transpose(1, 2, 0)
    tail = jnp.zeros((heads, BF16_SUBLANES, rows), v.dtype).at[:, 0, :].set(1.0)
    return jnp.concatenate([vt, tail], axis=1)


def kernel(x, c, ctx, c_ctx, ada_down, ada_up, ada_bias, norm_mix, norm_ffn, w_in, na_q_norm, na_k_norm, na_rpb, mla_cq_norm, mla_ckv_norm, mla_w_q_up, mla_w_kv_up, mla_q_norm, mla_k_norm, w_branch_na, w_branch_mla, w_out, ffn_w_up, ffn_conv_w, ffn_conv_b, ffn_w_down):
    batch, n_tok, d = x.shape
    assert batch == 1 and c.shape[0] == 1
    depth = w_in.shape[0]
    naw = NA_HEADS * NA_HEAD_DIM
    xs = x[0]
    cs = ctx[0]
    lat_tabs = _rope_tables(n_tok)
    ctx_tabs = _identity_tables(cs.shape[0])
    cond = jnp.zeros((8, d), F32).at[0].set(c[0]).at[1].set(c_ctx)
    sw = _stacked_weights(w_in, mla_w_q_up, mla_w_kv_up, w_branch_na, w_branch_mla, w_out, ffn_w_up, ffn_w_down)

    for l in range(depth):
        last = l == depth - 1
        lw = _layer_weights(l, sw, na_q_norm, na_k_norm, mla_cq_norm, mla_ckv_norm, mla_q_norm, mla_k_norm,
                            ffn_conv_w, ffn_conv_b)
        mods = _adaln(cond, ada_down, ada_up, ada_bias, l)
        sh1, sc1, g1, sh2, sc2, g2 = [mods[0, i * d:(i + 1) * d] for i in range(N_MOD)]
        csh1, csc1, cg1, csh2, csc2, cg2 = [mods[1, i * d:(i + 1) * d] for i in range(N_MOD)]

        h = _norm_mod(xs, norm_mix[l], sc1, sh1)
        hc = _norm_mod(cs, norm_mix[l], csc1, csh1)
        na_qk, na_v, mq, mk, mv = _project(h, lw, lat_tabs)
        c_qk, c_v, cmq, cmk, cmv = _project(hc, lw, ctx_tabs)

        y_na = _na_attention(na_qk, na_v, c_qk, c_v, _na_bias_table(na_rpb[l], cs.shape[0]))
        y_mla = _flash(mq, mk, mv, (cmk, cmv), name="mla_attention")
        u = _merge(h, y_na, y_mla, lw)
        xs = _residual_matmul(u, lw["w_out"], xs, g1, "out_proj")
        h2 = _norm_mod(xs, norm_ffn[l], sc2, sh2)
        xs = _residual_matmul(_ffn_up(h2, lw), lw["w_down"], xs, g2, "ffn_down")

        if not last:
            yc_na = _flash(_head_major(c_qk[:, :naw], NA_HEADS), _head_major(c_qk[:, naw:], NA_HEADS),
                           _head_major_vt(c_v, NA_HEADS), None, name="ctx_na_attention")
            yc_mla = _flash(cmq, cmk, cmv, None, name="ctx_mla_attention")
            uc = _merge(hc, yc_na, yc_mla, lw)
            cs = _residual_matmul(uc, lw["w_out"], cs, cg1, "ctx_out_proj")
            hc2 = _norm_mod(cs, norm_ffn[l], csc2, csh2)
            cs = _residual_matmul(_ffn_up(hc2, lw), lw["w_down"], cs, cg2, "ctx_ffn_down")
    return xs[None]
```

```python
import functools
from typing import NamedTuple

import numpy as np
import jax
import jax.numpy as jnp
from jax import lax
from jax.experimental import pallas as pl
from jax.experimental.pallas import tpu as pltpu

F32 = jnp.float32
BF16 = jnp.bfloat16

GRID_W = 64
EPS = 1e-6
NEG_INF = -1e30
N_MOD = 6

NA_HEADS = 16
NA_HEAD_DIM = 128
NA_WIN_H = 8
NA_WIN_W = 16
NA_GROUP = 2

MLA_HEADS = 16
MLA_NOPE_DIM = 128
MLA_ROPE_DIM = 64
MLA_QK_DIM = MLA_NOPE_DIM + MLA_ROPE_DIM
MLA_V_DIM = 128
ROPE_THETA = 10000.0
CONV_W = 3

LANES = 128
MLA_QK_PAD = 2 * LANES
BF16_SUBLANES = 16
VT_ROWS = MLA_V_DIM + BF16_SUBLANES
LOG2E = 1.4426950408889634
VMEM_LIMIT_BYTES = 56 * 1024 * 1024


def _tile(dim, pref, mult=8):
    if dim <= pref:
        return dim
    t = (pref // mult) * mult
    while t >= mult:
        if dim % t == 0:
            return t
        t -= mult
    return dim


def _params(*sem):
    return pltpu.CompilerParams(dimension_semantics=sem, vmem_limit_bytes=VMEM_LIMIT_BYTES)


def _ada_down_kernel(c_ref, w_ref, o_ref):
    c = c_ref[...]
    a = (c * jax.nn.sigmoid(c)).astype(BF16)
    o_ref[...] = jnp.dot(a, w_ref[...].astype(BF16), preferred_element_type=F32)


def _ada_up_kernel(t_ref, w_ref, b_ref, o_ref):
    o_ref[...] = jnp.dot(t_ref[...].astype(BF16), w_ref[...].astype(BF16),
                         preferred_element_type=F32) + b_ref[...]


def _adaln(cond, down, up, bias, layer):
    rows, d = cond.shape
    rank = down.shape[2]
    n = up.shape[2]
    bn1 = _tile(rank, 512, LANES)
    t = pl.pallas_call(
        _ada_down_kernel,
        grid=(rank // bn1,),
        in_specs=[pl.BlockSpec((rows, d), lambda j: (0, 0)),
                  pl.BlockSpec((None, d, bn1), lambda j: (layer, 0, j))],
        out_specs=pl.BlockSpec((rows, bn1), lambda j: (0, j)),
        out_shape=jax.ShapeDtypeStruct((rows, rank), F32),
        compiler_params=_params("parallel"),
        name="ada_down",
    )(cond, down)
    bn2 = _tile(n, 4096, LANES)
    return pl.pallas_call(
        _ada_up_kernel,
        grid=(n // bn2,),
        in_specs=[pl.BlockSpec((rows, rank), lambda j: (0, 0)),
                  pl.BlockSpec((None, rank, bn2), lambda j: (layer, 0, j)),
                  pl.BlockSpec((None, 1, bn2), lambda j: (layer, 0, j))],
        out_specs=pl.BlockSpec((rows, bn2), lambda j: (0, j)),
        out_shape=jax.ShapeDtypeStruct((rows, n), F32),
        compiler_params=_params("parallel"),
        name="ada_up",
    )(t, up, bias.reshape(bias.shape[0], 1, n))


def _norm_mod_kernel(x_ref, g_ref, sc_ref, sh_ref, o_ref):
    x = x_ref[...]
    y = x * lax.rsqrt(jnp.mean(x * x, axis=-1, keepdims=True) + EPS) * g_ref[...]
    o_ref[...] = (y * (1.0 + sc_ref[...]) + sh_ref[...]).astype(o_ref.dtype)


def _norm_mod(x, g, sc, sh):
    m, d = x.shape
    bm = _tile(m, 512)
    vec = pl.BlockSpec((1, d), lambda i: (0, 0))
    return pl.pallas_call(
        _norm_mod_kernel,
        grid=(m // bm,),
        in_specs=[pl.BlockSpec((bm, d), lambda i: (i, 0)), vec, vec, vec],
        out_specs=pl.BlockSpec((bm, d), lambda i: (i, 0)),
        out_shape=jax.ShapeDtypeStruct((m, d), BF16),
        compiler_params=_params("parallel"),
        name="norm_mod",
    )(x, g.reshape(1, d), sc.reshape(1, d), sh.reshape(1, d))


def _mm_kernel(x_ref, w_ref, *rest, epilogue):
    acc = jnp.dot(x_ref[...], w_ref[...], preferred_element_type=F32)
    epilogue(acc, *rest)


class _W(NamedTuple):
    arr: jax.Array
    layer: int
    col0: int
    n: int


def _w_spec(w, bn, shift=0, j_first=False):
    blk0, rem = divmod(w.col0, bn)
    assert rem == 0 and w.n % bn == 0
    if j_first:
        return pl.BlockSpec((None, w.arr.shape[1], bn), lambda j, i: (w.layer, 0, blk0 + shift + j))
    return pl.BlockSpec((None, w.arr.shape[1], bn), lambda i, j: (w.layer, 0, blk0 + shift + j))


def _matmul(x, w, epilogue, extras, extra_specs, out_shapes, out_specs, bm, bn, name):
    m, k = x.shape
    return pl.pallas_call(
        functools.partial(_mm_kernel, epilogue=epilogue),
        grid=(m // bm, w.n // bn),
        in_specs=[pl.BlockSpec((bm, k), lambda i, j: (i, 0)), _w_spec(w, bn)] + list(extra_specs),
        out_specs=out_specs,
        out_shape=out_shapes,
        compiler_params=_params("parallel", "parallel"),
        name=name,
    )(x, w.arr, *extras)


def _epi_cast(acc, o_ref):
    o_ref[...] = acc.astype(o_ref.dtype)


def _epi_headnorm(acc, g_ref, o_ref):
    for c in range(acc.shape[1] // LANES):
        sl = slice(c * LANES, (c + 1) * LANES)
        a = acc[:, sl]
        r = lax.rsqrt(jnp.mean(a * a, axis=-1, keepdims=True) + EPS)
        o_ref[:, sl] = (a * r * g_ref[:, sl]).astype(o_ref.dtype)


def _epi_rownorm(acc, g_ref, o_ref):
    r = lax.rsqrt(jnp.mean(acc * acc, axis=-1, keepdims=True) + EPS)
    o_ref[...] = (acc * r * g_ref[...]).astype(o_ref.dtype)


def _epi_ckv(acc, g_ref, ckv_ref, kr_ref, *, kv_rank):
    a = acc[:, :kv_rank]
    r = lax.rsqrt(jnp.mean(a * a, axis=-1, keepdims=True) + EPS)
    ckv_ref[...] = (a * r * g_ref[...]).astype(ckv_ref.dtype)
    kr_ref[...] = acc[:, kv_rank:]


def _epi_q_up(acc, c_ref, s_ref, gn_ref, gr_ref, gs_ref, o_ref, *, heads, scale):
    cg = c_ref[...] * gr_ref[...]
    sg = s_ref[...] * gs_ref[...]
    for h in range(heads):
        base = h * 3 * LANES
        nope = acc[:, base:base + LANES]
        rp = acc[:, base + LANES:base + 2 * LANES]
        sw = acc[:, base + 2 * LANES:base + 3 * LANES]
        ss = jnp.sum(nope * nope, axis=-1, keepdims=True) + jnp.sum(rp * rp, axis=-1, keepdims=True)
        r = lax.rsqrt(ss * (1.0 / MLA_QK_DIM) + EPS) * scale
        o_ref[h, :, 0:LANES] = (nope * r * gn_ref[...]).astype(o_ref.dtype)
        o_ref[h, :, LANES:2 * LANES] = ((rp * cg + sw * sg) * r).astype(o_ref.dtype)


def _ones_row_tile(cols, dtype):
    rows = lax.broadcasted_iota(jnp.int32, (BF16_SUBLANES, cols), 0)
    return jnp.where(rows == 0, 1.0, 0.0).astype(dtype)


def _epi_kv_up(acc, kr_ref, c_ref, s_ref, gn_ref, gr_ref, gs_ref, k_ref, vt_ref, *, heads):
    kr = kr_ref[:, 0:LANES]
    krs = kr_ref[:, LANES:2 * LANES]
    ssr = jnp.sum(kr * kr, axis=-1, keepdims=True)
    rope = kr * (c_ref[...] * gr_ref[...]) + krs * (s_ref[...] * gs_ref[...])
    for h in range(heads):
        base = h * 2 * LANES
        nope = acc[:, base:base + LANES]
        ss = jnp.sum(nope * nope, axis=-1, keepdims=True) + ssr
        r = lax.rsqrt(ss * (1.0 / MLA_QK_DIM) + EPS)
        k_ref[h, :, 0:LANES] = (nope * r * gn_ref[...]).astype(k_ref.dtype)
        k_ref[h, :, LANES:2 * LANES] = (rope * r).astype(k_ref.dtype)
        vt_ref[h, 0:MLA_V_DIM, :] = acc[:, base + LANES:base + 2 * LANES].T.astype(vt_ref.dtype)
        vt_ref[h, MLA_V_DIM:VT_ROWS, :] = _ones_row_tile(acc.shape[0], vt_ref.dtype)


def _epi_residual(acc, x_ref, g_ref, o_ref):
    o_ref[...] = x_ref[...] + g_ref[...] * acc


def _project(h, lw, tabs):
    m, d = h.shape
    bm = _tile(m, 1024)
    naw = NA_HEADS * NA_HEAD_DIM
    row = lambda n: pl.BlockSpec((1, n), lambda i, j: (0, 0))

    bn = _tile(2 * naw, 1024, LANES)
    na_qk = _matmul(h, lw["w_na_qk"], _epi_headnorm, [lw["g_na_qk"]],
                    [pl.BlockSpec((1, bn), lambda i, j: (0, j))],
                    jax.ShapeDtypeStruct((m, 2 * naw), BF16),
                    pl.BlockSpec((bm, bn), lambda i, j: (i, j)), bm, bn, "na_qk_proj")
    bn = _tile(naw, 1024, LANES)
    na_v = _matmul(h, lw["w_na_v"], _epi_cast, [], [],
                   jax.ShapeDtypeStruct((m, naw), BF16),
                   pl.BlockSpec((bm, bn), lambda i, j: (i, j)), bm, bn, "na_v_proj")

    q_rank = lw["w_cq"].n
    kv_rank = lw["w_ckv_kr"].n - 2 * LANES
    bmc = _tile(m, 512)
    cqn = _matmul(h, lw["w_cq"], _epi_rownorm, [lw["g_cq"]], [row(q_rank)],
                  jax.ShapeDtypeStruct((m, q_rank), BF16),
                  pl.BlockSpec((bmc, q_rank), lambda i, j: (i, 0)), bmc, q_rank, "cq_proj")
    ckvn, kr = _matmul(h, lw["w_ckv_kr"], functools.partial(_epi_ckv, kv_rank=kv_rank),
                       [lw["g_ckv"]], [row(kv_rank)],
                       (jax.ShapeDtypeStruct((m, kv_rank), BF16),
                        jax.ShapeDtypeStruct((m, 2 * LANES), F32)),
                       (pl.BlockSpec((bmc, kv_rank), lambda i, j: (i, 0)),
                        pl.BlockSpec((bmc, 2 * LANES), lambda i, j: (i, 0))),
                       bmc, kv_rank + 2 * LANES, "ckv_proj")

    cos_t, sin_t = tabs
    hb = 4 if MLA_HEADS % 4 == 0 else 1
    bmq = _tile(m, 1024)
    tab = pl.BlockSpec((bmq, LANES), lambda i, j: (i, 0))
    vec = pl.BlockSpec((1, LANES), lambda i, j: (0, 0))
    mla_q = _matmul(cqn, lw["w_q_up"],
                    functools.partial(_epi_q_up, heads=hb, scale=MLA_QK_DIM ** -0.5 * LOG2E),
                    [cos_t, sin_t, lw["gq_nope"], lw["gq_rope"], lw["gq_swap"]],
                    [tab, tab, vec, vec, vec],
                    jax.ShapeDtypeStruct((MLA_HEADS, m, MLA_QK_PAD), BF16),
                    pl.BlockSpec((hb, bmq, MLA_QK_PAD), lambda i, j: (j, i, 0)),
                    bmq, hb * 3 * LANES, "mla_q_up")
    bmk = _tile(m, 1024)
    tab = pl.BlockSpec((bmk, LANES), lambda i, j: (i, 0))
    mla_k, mla_v = _matmul(ckvn, lw["w_kv_up"], functools.partial(_epi_kv_up, heads=hb),
                           [kr, cos_t, sin_t, lw["gk_nope"], lw["gk_rope"], lw["gk_swap"]],
                           [pl.BlockSpec((bmk, 2 * LANES), lambda i, j: (i, 0)), tab, tab, vec, vec, vec],
                           (jax.ShapeDtypeStruct((MLA_HEADS, m, MLA_QK_PAD), BF16),
                            jax.ShapeDtypeStruct((MLA_HEADS, VT_ROWS, m), BF16)),
                           (pl.BlockSpec((hb, bmk, MLA_QK_PAD), lambda i, j: (j, i, 0)),
                            pl.BlockSpec((hb, VT_ROWS, bmk), lambda i, j: (j, 0, i))),
                           bmk, hb * 2 * LANES, "mla_kv_up")
    return na_qk, na_v, mla_q, mla_k, mla_v


def _dot_nt(a, b):
    return lax.dot_general(a, b, (((1,), (1,)), ((), ())), preferred_element_type=F32)


def _softmax_pv(s_t, vt_blk, state):
    blk_max = jnp.max(s_t, axis=0, keepdims=True)
    m_new = blk_max if state is None else jnp.maximum(state[0], blk_max)
    p_t = jnp.exp2(s_t - m_new).astype(BF16)
    upd = jnp.dot(vt_blk, p_t, preferred_element_type=F32)
    if state is None:
        return m_new, upd
    return m_new, jnp.exp2(state[0] - m_new) * state[1] + upd


def _flash_kernel(*refs, tq, tk, n_chunks, has_ctx):
    if has_ctx:
        q_ref, k_ref, vt_ref, kc_ref, vct_ref, o_ref, s_scr = refs
    else:
        q_ref, k_ref, vt_ref, o_ref, s_scr = refs
    n_sub = q_ref.shape[0] // tq
    sub_rows = [slice(i * tq, (i + 1) * tq) for i in range(n_sub)]
    qs = [q_ref[r, :] for r in sub_rows]

    def chunk(c):
        return pl.ds(pl.multiple_of(c * tk, tk), tk)

    def scores(c, slot):
        k_blk = k_ref[chunk(c), :]
        for i, q in enumerate(qs):
            s_scr[slot, i, :, 0:tq] = _dot_nt(k_blk, q)

    def consume(c, slot, state):
        vt_blk = vt_ref[:, chunk(c)]
        return tuple(_softmax_pv(s_scr[slot, i, :, 0:tq], vt_blk, st) for i, st in enumerate(state))

    first = 0 if has_ctx else 1
    n_pairs = (n_chunks - first) // 2
    if has_ctx:
        init_k, init_vt = kc_ref[...], vct_ref[...]
    else:
        init_k, init_vt = k_ref[0:tk, :], vt_ref[:, 0:tk]
    init_scores = [_dot_nt(init_k, q) for q in qs]
    if n_pairs > 0:
        scores(first, 0)
    state = tuple(_softmax_pv(s, init_vt, None) for s in init_scores)

    if n_pairs > 0:

        def pair(j, state, prefetch):
            c0 = first + 2 * j
            scores(c0 + 1, 1)
            state = consume(c0, 0, state)
            if prefetch:
                scores(c0 + 2, 0)
            return consume(c0 + 1, 1, state)

        state = lax.fori_loop(0, n_pairs - 1, lambda j, st: pair(j, st, True), state)
        state = pair(n_pairs - 1, state, False)
    if (n_chunks - first) % 2:
        last = slice((n_chunks - 1) * tk, n_chunks * tk)
        state = tuple(_softmax_pv(_dot_nt(k_ref[last, :], q), vt_ref[:, last], st) for q, st in zip(qs, state))

    for rows, (_, acc) in zip(sub_rows, state):
        o_t = acc[0:MLA_V_DIM, :] / acc[MLA_V_DIM:MLA_V_DIM + 1, :]
        o_ref[rows, :] = o_t.T.astype(o_ref.dtype)


def _flash(q, k, vt, ctx_kv, *, name):
    heads, mq, dq = q.shape
    mk = k.shape[1]
    bq = _tile(mq, 512)
    tq = _tile(bq, 512)
    tk = _tile(mk, 2048, LANES)
    q_spec = pl.BlockSpec((None, bq, dq), lambda h, i: (h, i, 0))
    whole = lambda a: pl.BlockSpec((None,) + a.shape[1:], lambda h, i: (h, 0, 0))
    in_specs = [q_spec, whole(k), whole(vt)]
    args = [q, k, vt]
    if ctx_kv is not None:
        in_specs += [whole(ctx_kv[0]), whole(ctx_kv[1])]
        args += list(ctx_kv)
    return pl.pallas_call(
        functools.partial(_flash_kernel, tq=tq, tk=tk, n_chunks=mk // tk, has_ctx=ctx_kv is not None),
        grid=(heads, mq // bq),
        in_specs=in_specs,
        out_specs=pl.BlockSpec((bq, MLA_V_DIM), lambda h, i: (i, h)),
        out_shape=jax.ShapeDtypeStruct((mq, heads * MLA_V_DIM), BF16),
        scratch_shapes=[pltpu.VMEM((2, bq // tq, tk, tq + LANES), F32)],
        compiler_params=_params("parallel", "parallel"),
        name=name,
    )(*args)


def _na_kernel(q_ref, k_ref, v_ref, kc_ref, vc_ref, b_ref, o_ref):
    head = lambda h: slice(h * NA_HEAD_DIM, (h + 1) * NA_HEAD_DIM)

    def group_scores(g):
        return [_dot_nt(q_ref[:, head(h)], jnp.concatenate([k_ref[:, head(h)], kc_ref[:, head(h)]], axis=0))
                + b_ref[0, h] for h in range(g * NA_GROUP, (g + 1) * NA_GROUP)]

    n_groups = NA_HEADS // NA_GROUP
    outs = []
    nxt = group_scores(0)
    for g in range(n_groups):
        cur = nxt
        if g + 1 < n_groups:
            nxt = group_scores(g + 1)
        probs, inv_l = [], []
        for s in cur:
            p = jnp.exp2(s - jnp.max(s, axis=-1, keepdims=True))
            inv_l.append(1.0 / jnp.sum(p, axis=-1, keepdims=True))
            probs.append(p.astype(BF16))
        for i, h in enumerate(range(g * NA_GROUP, (g + 1) * NA_GROUP)):
            v_all = jnp.concatenate([v_ref[:, head(h)], vc_ref[:, head(h)]], axis=0)
            outs.append(jnp.dot(probs[i], v_all, preferred_element_type=F32) * inv_l[i])
    o_ref[...] = jnp.concatenate(outs, axis=-1).astype(o_ref.dtype)


def _na_bias_table(rpb, n_ctx):
    qc = np.arange(GRID_W)
    kc = np.arange(GRID_W)
    cs = np.clip(qc - NA_WIN_W // 2, 0, GRID_W - NA_WIN_W)
    col_ok = (kc[None, :] >= cs[:, None]) & (kc[None, :] < cs[:, None] + NA_WIN_W)
    col_idx = np.clip(kc[None, :] - qc[:, None] + NA_WIN_W - 1, 0, 2 * NA_WIN_W - 2)
    cols = rpb.astype(F32)[:, :, col_idx]
    cols = jnp.where(jnp.asarray(col_ok)[None, None], cols * LOG2E, NEG_INF)
    variants = [cols[:, v:v + NA_WIN_H].transpose(0, 2, 1, 3).reshape(NA_HEADS, GRID_W, NA_WIN_H * GRID_W)
                for v in range(NA_WIN_H)]
    table = jnp.stack(variants)
    return jnp.concatenate([table, jnp.zeros(table.shape[:3] + (n_ctx,), F32)], axis=-1)


def _na_attention(na_qk, na_v, ctx_qk, ctx_v, bias):
    m = na_qk.shape[0]
    naw = NA_HEADS * NA_HEAD_DIM
    rows_n = m // GRID_W
    win = NA_WIN_H * GRID_W
    n_ctx = ctx_qk.shape[0]

    def start(r):
        return jnp.clip(r - NA_WIN_H // 2, 0, rows_n - NA_WIN_H)

    E = pl.Element
    return pl.pallas_call(
        _na_kernel,
        grid=(rows_n,),
        in_specs=[
            pl.BlockSpec((GRID_W, naw), lambda r: (r, 0)),
            pl.BlockSpec((E(win), E(naw)), lambda r: (start(r) * GRID_W, naw)),
            pl.BlockSpec((E(win), E(naw)), lambda r: (start(r) * GRID_W, 0)),
            pl.BlockSpec((n_ctx, naw), lambda r: (0, 1)),
            pl.BlockSpec((n_ctx, naw), lambda r: (0, 0)),
            pl.BlockSpec((1, NA_HEADS, GRID_W, win + n_ctx), lambda r: (start(r) - r + NA_WIN_H - 1, 0, 0, 0)),
        ],
        out_specs=pl.BlockSpec((GRID_W, naw), lambda r: (r, 0)),
        out_shape=jax.ShapeDtypeStruct((m, naw), BF16),
        compiler_params=_params("parallel"),
        name="na_attention",
    )(na_qk, na_qk, na_v, ctx_qk, ctx_v, bias)


def _merge_kernel(h_ref, yn_ref, ym_ref, wgn_ref, wgm_ref, wbn_ref, wbm_ref, o_ref):
    h = h_ref[...]
    gn = jax.nn.sigmoid(jnp.dot(h, wgn_ref[...], preferred_element_type=F32))
    gm = jax.nn.sigmoid(jnp.dot(h, wgm_ref[...], preferred_element_type=F32))
    a = jnp.dot(yn_ref[...], wbn_ref[...], preferred_element_type=F32)
    b = jnp.dot(ym_ref[...], wbm_ref[...], preferred_element_type=F32)
    o_ref[...] = (gn * a + gm * b).astype(o_ref.dtype)


def _merge(h, y_na, y_mla, lw):
    m, d = h.shape
    bm = _tile(m, 512)
    bn = _tile(d, 512, LANES)
    nj = d // bn
    kn, km = y_na.shape[1], y_mla.shape[1]
    return pl.pallas_call(
        _merge_kernel,
        grid=(nj, m // bm),
        in_specs=[pl.BlockSpec((bm, d), lambda j, i: (i, 0)),
                  pl.BlockSpec((bm, kn), lambda j, i: (i, 0)),
                  pl.BlockSpec((bm, km), lambda j, i: (i, 0)),
                  _w_spec(lw["w_gates"], bn, j_first=True),
                  _w_spec(lw["w_gates"], bn, shift=nj, j_first=True),
                  _w_spec(lw["w_b_na"], bn, j_first=True),
                  _w_spec(lw["w_b_mla"], bn, j_first=True)],
        out_specs=pl.BlockSpec((bm, bn), lambda j, i: (i, j)),
        out_shape=jax.ShapeDtypeStruct((m, d), BF16),
        compiler_params=_params("parallel", "parallel"),
        name="branch_merge",
    )(h, y_na, y_mla, lw["w_gates"].arr, lw["w_gates"].arr, lw["w_b_na"].arr, lw["w_b_mla"].arr)


def _residual_matmul(a, w, x, gate, name):
    m, d = x.shape
    bm = _tile(m, 1024)
    bn = _tile(d, 512, LANES)
    return _matmul(a, w, _epi_residual, [x, gate.reshape(1, d)],
                   [pl.BlockSpec((bm, bn), lambda i, j: (i, j)),
                    pl.BlockSpec((1, bn), lambda i, j: (0, j))],
                   jax.ShapeDtypeStruct((m, d), F32),
                   pl.BlockSpec((bm, bn), lambda i, j: (i, j)), bm, bn, name)


def _ffn_up_kernel(x_ref, wg_ref, wv_ref, halo_ref, cw_ref, cb_ref, o_ref):
    x = x_ref[...]
    g = jnp.dot(x, wg_ref[...], preferred_element_type=F32)
    bm = g.shape[0]
    rows = lax.broadcasted_iota(jnp.int32, g.shape, 0)
    g_prev = jnp.where(rows == 0, halo_ref[0:1, :], pltpu.roll(g, 1, axis=0))
    g_next = jnp.where(rows == bm - 1, halo_ref[1:2, :], pltpu.roll(g, bm - 1, axis=0))
    y = g_prev * cw_ref[0:1, :] + g * cw_ref[1:2, :] + g_next * cw_ref[2:3, :] + cb_ref[...]
    v = jnp.dot(x, wv_ref[...], preferred_element_type=F32)
    o_ref[...] = (y * jax.nn.sigmoid(y) * v).astype(o_ref.dtype)


def _ffn_up(h2, lw):
    m, d = h2.shape
    dff = lw["conv_b"].shape[1]
    bm = _tile(m, 1024)
    bn = _tile(dff, 512, LANES)
    nt, nj = m // bm, dff // bn

    edge = h2.reshape(nt, bm, d)[:, (0, bm - 1), :].reshape(2 * nt, d)
    pad = (-edge.shape[0]) % 16
    edge = jnp.pad(edge, ((0, pad), (0, 0)))
    bne = _tile(dff, 1024, LANES)
    eg = _matmul(edge, lw["w_up"]._replace(n=dff), _epi_cast, [], [],
                 jax.ShapeDtypeStruct((edge.shape[0], dff), F32),
                 pl.BlockSpec((edge.shape[0], bne), lambda i, j: (i, j)),
                 edge.shape[0], bne, "ffn_edge_rows")
    eg = eg[:2 * nt].reshape(nt, 2, dff)
    zero = jnp.zeros((1, dff), F32)
    prev_rows = jnp.concatenate([zero, eg[:-1, 1]], axis=0)
    next_rows = jnp.concatenate([eg[1:, 0], zero], axis=0)
    halo = jnp.stack([prev_rows, next_rows], axis=1)

    return pl.pallas_call(
        _ffn_up_kernel,
        grid=(m // bm, nj),
        in_specs=[pl.BlockSpec((bm, d), lambda i, j: (i, 0)),
                  _w_spec(lw["w_up"], bn),
                  _w_spec(lw["w_up"], bn, shift=nj),
                  pl.BlockSpec((None, 2, bn), lambda i, j: (i, 0, j)),
                  pl.BlockSpec((CONV_W, bn), lambda i, j: (0, j)),
                  pl.BlockSpec((1, bn), lambda i, j: (0, j))],
        out_specs=pl.BlockSpec((bm, bn), lambda i, j: (i, j)),
        out_shape=jax.ShapeDtypeStruct((m, dff), BF16),
        compiler_params=_params("parallel", "parallel"),
        name="ffn_up_conv",
    )(h2, lw["w_up"].arr, lw["w_up"].arr, halo, lw["conv_w"], lw["conv_b"])


def _rope_partner():
    half = MLA_ROPE_DIM // 2
    quarter = half // 2
    idx = np.arange(MLA_ROPE_DIM)
    return np.where((idx % half) < quarter, idx + quarter, idx - quarter)


def _pad_lanes(a):
    return jnp.pad(a, [(0, 0)] * (a.ndim - 1) + [(0, LANES - a.shape[-1])])


def _stacked_weights(w_in, mla_w_q_up, mla_w_kv_up, w_branch_na, w_branch_mla, w_out, ffn_w_up, ffn_w_down):
    depth = w_in.shape[0]
    naw = NA_HEADS * NA_HEAD_DIM
    q_rank = mla_w_q_up.shape[1]
    kv_rank = mla_w_kv_up.shape[1]
    perm = _rope_partner()
    o_cq = 3 * naw
    o_ckv = o_cq + q_rank
    o_kr = o_ckv + kv_rank
    o_g = o_kr + MLA_ROPE_DIM
    w_in = w_in.astype(BF16)
    w_kr = w_in[:, :, o_kr:o_g]
    wq = mla_w_q_up.reshape(depth, q_rank, MLA_HEADS, MLA_QK_DIM)
    wq_rope = wq[..., MLA_NOPE_DIM:]
    return {
        "in_head": w_in,
        "ckv_kr": jnp.concatenate([w_in[:, :, o_ckv:o_kr], _pad_lanes(w_kr), _pad_lanes(w_kr[:, :, perm])], axis=2),
        "gates": w_in[:, :, o_g:],
        "q_up": jnp.concatenate([wq[..., :MLA_NOPE_DIM], _pad_lanes(wq_rope), _pad_lanes(wq_rope[..., perm])],
                                axis=3).reshape(depth, q_rank, MLA_HEADS * 3 * LANES).astype(BF16),
        "kv_up": mla_w_kv_up.astype(BF16),
        "b_na": w_branch_na.astype(BF16),
        "b_mla": w_branch_mla.astype(BF16),
        "out": w_out.astype(BF16),
        "up": ffn_w_up.astype(BF16),
        "down": ffn_w_down.astype(BF16),
    }


def _layer_weights(l, sw, na_q_norm, na_k_norm, mla_cq_norm, mla_ckv_norm, mla_q_norm, mla_k_norm,
                   ffn_conv_w, ffn_conv_b):
    naw = NA_HEADS * NA_HEAD_DIM
    q_rank = sw["q_up"].shape[1]
    kv_rank = sw["kv_up"].shape[1]
    perm = _rope_partner()
    whole = lambda a: _W(a, l, 0, a.shape[2])
    lw = {
        "w_na_qk": _W(sw["in_head"], l, 0, 2 * naw),
        "w_na_v": _W(sw["in_head"], l, 2 * naw, naw),
        "w_cq": _W(sw["in_head"], l, 3 * naw, q_rank),
        "w_ckv_kr": whole(sw["ckv_kr"]),
        "w_gates": whole(sw["gates"]),
        "w_q_up": whole(sw["q_up"]),
        "w_kv_up": whole(sw["kv_up"]),
        "w_b_na": whole(sw["b_na"]),
        "w_b_mla": whole(sw["b_mla"]),
        "w_out": whole(sw["out"]),
        "w_up": whole(sw["up"]),
        "w_down": whole(sw["down"]),
        "g_na_qk": jnp.concatenate([jnp.tile(na_q_norm[l] * (NA_HEAD_DIM ** -0.5 * LOG2E), NA_HEADS),
                                    jnp.tile(na_k_norm[l], NA_HEADS)]).reshape(1, 2 * naw),
        "g_cq": mla_cq_norm[l].reshape(1, q_rank),
        "g_ckv": mla_ckv_norm[l].reshape(1, kv_rank),
        "conv_w": ffn_conv_w[l],
        "conv_b": ffn_conv_b[l].reshape(1, -1),
    }
    for tag, g in (("q", mla_q_norm[l]), ("k", mla_k_norm[l])):
        g_rope = g[MLA_NOPE_DIM:]
        lw[f"g{tag}_nope"] = g[:MLA_NOPE_DIM].reshape(1, LANES)
        lw[f"g{tag}_rope"] = _pad_lanes(g_rope).reshape(1, LANES)
        lw[f"g{tag}_swap"] = _pad_lanes(g_rope[perm]).reshape(1, LANES)
    return lw


def _rope_tables(n_tok):
    t = jnp.arange(n_tok, dtype=jnp.int32)
    half = MLA_ROPE_DIM // 2
    inv_freq = ROPE_THETA ** (-jnp.arange(0, half, 2, dtype=F32) / half)
    cs, sn = [], []
    for pos in (t // GRID_W, t % GRID_W):
        ang = pos.astype(F32)[:, None] * inv_freq[None, :]
        cs += [jnp.cos(ang), jnp.cos(ang)]
        sn += [-jnp.sin(ang), jnp.sin(ang)]
    return _pad_lanes(jnp.concatenate(cs, axis=1)), _pad_lanes(jnp.concatenate(sn, axis=1))


def _identity_tables(n_tok):
    cos_t = _pad_lanes(jnp.ones((n_tok, MLA_ROPE_DIM), F32))
    return cos_t, jnp.zeros((n_tok, LANES), F32)


def _head_major(a, heads):
    rows = a.shape[0]
    return a.reshape(rows, heads, -1).transpose(1, 0, 2)


def _head_major_vt(v, heads):
    rows = v.shape[0]
    vt = v.reshape(rows, heads, MLA_V_DIM).transpose(1, 2, 0)
    tail = jnp.zeros((heads, BF16_SUBLANES, rows), v.dtype).at[:, 0, :].set(1.0)
    return jnp.concatenate([vt, tail], axis=1)


def kernel(x, c, ctx, c_ctx, ada_down, ada_up, ada_bias, norm_mix, norm_ffn, w_in, na_q_norm, na_k_norm, na_rpb, mla_cq_norm, mla_ckv_norm, mla_w_q_up, mla_w_kv_up, mla_q_norm, mla_k_norm, w_branch_na, w_branch_mla, w_out, ffn_w_up, ffn_conv_w, ffn_conv_b, ffn_w_down):
    batch, n_tok, d = x.shape
    assert batch == 1 and c.shape[0] == 1
    depth = w_in.shape[0]
    naw = NA_HEADS * NA_HEAD_DIM
    xs = x[0]
    cs = ctx[0]
    lat_tabs = _rope_tables(n_tok)
    ctx_tabs = _identity_tables(cs.shape[0])
    cond = jnp.zeros((8, d), F32).at[0].set(c[0]).at[1].set(c_ctx)
    sw = _stacked_weights(w_in, mla_w_q_up, mla_w_kv_up, w_branch_na, w_branch_mla, w_out, ffn_w_up, ffn_w_down)

    for l in range(depth):
        last = l == depth - 1
        lw = _layer_weights(l, sw, na_q_norm, na_k_norm, mla_cq_norm, mla_ckv_norm, mla_q_norm, mla_k_norm,
                            ffn_conv_w, ffn_conv_b)
        mods = _adaln(cond, ada_down, ada_up, ada_bias, l)
        sh1, sc1, g1, sh2, sc2, g2 = [mods[0, i * d:(i + 1) * d] for i in range(N_MOD)]
        csh1, csc1, cg1, csh2, csc2, cg2 = [mods[1, i * d:(i + 1) * d] for i in range(N_MOD)]

        h = _norm_mod(xs, norm_mix[l], sc1, sh1)
        hc = _norm_mod(cs, norm_mix[l], csc1, csh1)
        na_qk, na_v, mq, mk, mv = _project(h, lw, lat_tabs)
        c_qk, c_v, cmq, cmk, cmv = _project(hc, lw, ctx_tabs)

        y_na = _na_attention(na_qk, na_v, c_qk, c_v, _na_bias_table(na_rpb[l], cs.shape[0]))
        y_mla = _flash(mq, mk, mv, (cmk, cmv), name="mla_attention")
        u = _merge(h, y_na, y_mla, lw)
        xs = _residual_matmul(u, lw["w_out"], xs, g1, "out_proj")
        h2 = _norm_mod(xs, norm_ffn[l], sc2, sh2)
        xs = _residual_matmul(_ffn_up(h2, lw), lw["w_down"], xs, g2, "ffn_down")

        if not last:
            yc_na = _flash(_head_major(c_qk[:, :naw], NA_HEADS), _head_major(c_qk[:, naw:], NA_HEADS),
                           _head_major_vt(c_v, NA_HEADS), None, name="ctx_na_attention")
            yc_mla = _flash(cmq, cmk, cmv, None, name="ctx_mla_attention")
            uc = _merge(hc, yc_na, yc_mla, lw)
            cs = _residual_matmul(uc, lw["w_out"], cs, cg1, "ctx_out_proj")
            hc2 = _norm_mod(cs, norm_ffn[l], csc2, csh2)
            cs = _residual_matmul(_ffn_up(hc2, lw), lw["w_down"], cs, cg2, "ctx_ffn_down")
    return xs[None]
```

```python
import functools
from typing import NamedTuple

import numpy as np
import jax
import jax.numpy as jnp
from jax import lax
from jax.experimental import pallas as pl
from jax.experimental.pallas import tpu as pltpu

F32 = jnp.float32
BF16 = jnp.bfloat16

GRID_W = 64
EPS = 1e-6
NEG_INF = -1e30
N_MOD = 6

NA_HEADS = 16
NA_HEAD_DIM = 128
NA_WIN_H = 8
NA_WIN_W = 16
NA_GROUP = 1
NA_ROWS = 2

MLA_HEADS = 16
MLA_NOPE_DIM = 128
MLA_ROPE_DIM = 64
MLA_QK_DIM = MLA_NOPE_DIM + MLA_ROPE_DIM
MLA_V_DIM = 128
ROPE_THETA = 10000.0
CONV_W = 3

LANES = 128
MLA_QK_PAD = 2 * LANES
BF16_SUBLANES = 16
VT_ROWS = MLA_V_DIM + BF16_SUBLANES
LOG2E = 1.4426950408889634
VMEM_LIMIT_BYTES = 56 * 1024 * 1024


def _tile(dim, pref, mult=8):
    if dim <= pref:
        return dim
    t = (pref // mult) * mult
    while t >= mult:
        if dim % t == 0:
            return t
        t -= mult
    return dim


def _params(*sem):
    return pltpu.CompilerParams(dimension_semantics=sem, vmem_limit_bytes=VMEM_LIMIT_BYTES)


def _ada_down_kernel(c_ref, w_ref, o_ref):
    c = c_ref[...]
    a = (c * jax.nn.sigmoid(c)).astype(BF16)
    o_ref[...] = jnp.dot(a, w_ref[...].astype(BF16), preferred_element_type=F32)


def _ada_up_kernel(t_ref, w_ref, b_ref, o_ref):
    o_ref[...] = jnp.dot(t_ref[...].astype(BF16), w_ref[...].astype(BF16),
                         preferred_element_type=F32) + b_ref[...]


def _adaln(cond, down, up, bias, layer):
    rows, d = cond.shape
    rank = down.shape[2]
    n = up.shape[2]
    bn1 = _tile(rank, 512, LANES)
    t = pl.pallas_call(
        _ada_down_kernel,
        grid=(rank // bn1,),
        in_specs=[pl.BlockSpec((rows, d), lambda j: (0, 0)),
                  pl.BlockSpec((None, d, bn1), lambda j: (layer, 0, j))],
        out_specs=pl.BlockSpec((rows, bn1), lambda j: (0, j)),
        out_shape=jax.ShapeDtypeStruct((rows, rank), F32),
        compiler_params=_params("parallel"),
        name="ada_down",
    )(cond, down)
    bn2 = _tile(n, 4096, LANES)
    return pl.pallas_call(
        _ada_up_kernel,
        grid=(n // bn2,),
        in_specs=[pl.BlockSpec((rows, rank), lambda j: (0, 0)),
                  pl.BlockSpec((None, rank, bn2), lambda j: (layer, 0, j)),
                  pl.BlockSpec((None, 1, bn2), lambda j: (layer, 0, j))],
        out_specs=pl.BlockSpec((rows, bn2), lambda j: (0, j)),
        out_shape=jax.ShapeDtypeStruct((rows, n), F32),
        compiler_params=_params("parallel"),
        name="ada_up",
    )(t, up, bias.reshape(bias.shape[0], 1, n))


def _norm_mod_kernel(x_ref, g_ref, sc_ref, sh_ref, o_ref):
    x = x_ref[...]
    y = x * lax.rsqrt(jnp.mean(x * x, axis=-1, keepdims=True) + EPS) * g_ref[...]
    o_ref[...] = (y * (1.0 + sc_ref[...]) + sh_ref[...]).astype(o_ref.dtype)


def _norm_mod(x, g, sc, sh):
    m, d = x.shape
    bm = _tile(m, 512)
    vec = pl.BlockSpec((1, d), lambda i: (0, 0))
    return pl.pallas_call(
        _norm_mod_kernel,
        grid=(m // bm,),
        in_specs=[pl.BlockSpec((bm, d), lambda i: (i, 0)), vec, vec, vec],
        out_specs=pl.BlockSpec((bm, d), lambda i: (i, 0)),
        out_shape=jax.ShapeDtypeStruct((m, d), BF16),
        compiler_params=_params("parallel"),
        name="norm_mod",
    )(x, g.reshape(1, d), sc.reshape(1, d), sh.reshape(1, d))


def _mm_kernel(x_ref, w_ref, *rest, epilogue):
    acc = jnp.dot(x_ref[...], w_ref[...], preferred_element_type=F32)
    epilogue(acc, *rest)


class _W(NamedTuple):
    arr: jax.Array
    layer: int
    col0: int
    n: int


def _w_spec(w, bn, shift=0, j_first=False):
    blk0, rem = divmod(w.col0, bn)
    assert rem == 0 and w.n % bn == 0
    if j_first:
        return pl.BlockSpec((None, w.arr.shape[1], bn), lambda j, i: (w.layer, 0, blk0 + shift + j))
    return pl.BlockSpec((None, w.arr.shape[1], bn), lambda i, j: (w.layer, 0, blk0 + shift + j))


def _matmul(x, w, epilogue, extras, extra_specs, out_shapes, out_specs, bm, bn, name):
    m, k = x.shape
    return pl.pallas_call(
        functools.partial(_mm_kernel, epilogue=epilogue),
        grid=(m // bm, w.n // bn),
        in_specs=[pl.BlockSpec((bm, k), lambda i, j: (i, 0)), _w_spec(w, bn)] + list(extra_specs),
        out_specs=out_specs,
        out_shape=out_shapes,
        compiler_params=_params("parallel", "parallel"),
        name=name,
    )(x, w.arr, *extras)


def _epi_cast(acc, o_ref):
    o_ref[...] = acc.astype(o_ref.dtype)


def _epi_headnorm(acc, g_ref, o_ref):
    for c in range(acc.shape[1] // LANES):
        sl = slice(c * LANES, (c + 1) * LANES)
        a = acc[:, sl]
        r = lax.rsqrt(jnp.mean(a * a, axis=-1, keepdims=True) + EPS)
        o_ref[:, sl] = (a * r * g_ref[:, sl]).astype(o_ref.dtype)


def _epi_rownorm(acc, g_ref, o_ref):
    r = lax.rsqrt(jnp.mean(acc * acc, axis=-1, keepdims=True) + EPS)
    o_ref[...] = (acc * r * g_ref[...]).astype(o_ref.dtype)


def _epi_ckv(acc, g_ref, ckv_ref, kr_ref, *, kv_rank):
    a = acc[:, :kv_rank]
    r = lax.rsqrt(jnp.mean(a * a, axis=-1, keepdims=True) + EPS)
    ckv_ref[...] = (a * r * g_ref[...]).astype(ckv_ref.dtype)
    kr_ref[...] = acc[:, kv_rank:]


def _epi_q_up(acc, c_ref, s_ref, gn_ref, gr_ref, gs_ref, o_ref, *, heads, scale):
    cg = c_ref[...] * gr_ref[...]
    sg = s_ref[...] * gs_ref[...]
    for h in range(heads):
        base = h * 3 * LANES
        nope = acc[:, base:base + LANES]
        rp = acc[:, base + LANES:base + 2 * LANES]
        sw = acc[:, base + 2 * LANES:base + 3 * LANES]
        ss = jnp.sum(nope * nope, axis=-1, keepdims=True) + jnp.sum(rp * rp, axis=-1, keepdims=True)
        r = lax.rsqrt(ss * (1.0 / MLA_QK_DIM) + EPS) * scale
        o_ref[h, :, 0:LANES] = (nope * r * gn_ref[...]).astype(o_ref.dtype)
        o_ref[h, :, LANES:2 * LANES] = ((rp * cg + sw * sg) * r).astype(o_ref.dtype)


def _ones_row_tile(cols, dtype):
    rows = lax.broadcasted_iota(jnp.int32, (BF16_SUBLANES, cols), 0)
    return jnp.where(rows == 0, 1.0, 0.0).astype(dtype)


def _epi_kv_up(acc, kr_ref, c_ref, s_ref, gn_ref, gr_ref, gs_ref, k_ref, vt_ref, *, heads):
    kr = kr_ref[:, 0:LANES]
    krs = kr_ref[:, LANES:2 * LANES]
    ssr = jnp.sum(kr * kr, axis=-1, keepdims=True)
    rope = kr * (c_ref[...] * gr_ref[...]) + krs * (s_ref[...] * gs_ref[...])
    for h in range(heads):
        base = h * 2 * LANES
        nope = acc[:, base:base + LANES]
        ss = jnp.sum(nope * nope, axis=-1, keepdims=True) + ssr
        r = lax.rsqrt(ss * (1.0 / MLA_QK_DIM) + EPS)
        k_ref[h, :, 0:LANES] = (nope * r * gn_ref[...]).astype(k_ref.dtype)
        k_ref[h, :, LANES:2 * LANES] = (rope * r).astype(k_ref.dtype)
        vt_ref[h, 0:MLA_V_DIM, :] = acc[:, base + LANES:base + 2 * LANES].T.astype(vt_ref.dtype)
        vt_ref[h, MLA_V_DIM:VT_ROWS, :] = _ones_row_tile(acc.shape[0], vt_ref.dtype)


def _epi_residual(acc, x_ref, g_ref, o_ref):
    o_ref[...] = x_ref[...] + g_ref[...] * acc


def _project(h, lw, tabs):
    m, d = h.shape
    bm = _tile(m, 1024)
    naw = NA_HEADS * NA_HEAD_DIM
    row = lambda n: pl.BlockSpec((1, n), lambda i, j: (0, 0))

    bn = _tile(2 * naw, 1024, LANES)
    na_qk = _matmul(h, lw["w_na_qk"], _epi_headnorm, [lw["g_na_qk"]],
                    [pl.BlockSpec((1, bn), lambda i, j: (0, j))],
                    jax.ShapeDtypeStruct((m, 2 * naw), BF16),
                    pl.BlockSpec((bm, bn), lambda i, j: (i, j)), bm, bn, "na_qk_proj")
    bn = _tile(naw, 1024, LANES)
    na_v = _matmul(h, lw["w_na_v"], _epi_cast, [], [],
                   jax.ShapeDtypeStruct((m, naw), BF16),
                   pl.BlockSpec((bm, bn), lambda i, j: (i, j)), bm, bn, "na_v_proj")

    q_rank = lw["w_cq"].n
    kv_rank = lw["w_ckv_kr"].n - 2 * LANES
    bmc = _tile(m, 512)
    cqn = _matmul(h, lw["w_cq"], _epi_rownorm, [lw["g_cq"]], [row(q_rank)],
                  jax.ShapeDtypeStruct((m, q_rank), BF16),
                  pl.BlockSpec((bmc, q_rank), lambda i, j: (i, 0)), bmc, q_rank, "cq_proj")
    ckvn, kr = _matmul(h, lw["w_ckv_kr"], functools.partial(_epi_ckv, kv_rank=kv_rank),
                       [lw["g_ckv"]], [row(kv_rank)],
                       (jax.ShapeDtypeStruct((m, kv_rank), BF16),
                        jax.ShapeDtypeStruct((m, 2 * LANES), F32)),
                       (pl.BlockSpec((bmc, kv_rank), lambda i, j: (i, 0)),
                        pl.BlockSpec((bmc, 2 * LANES), lambda i, j: (i, 0))),
                       bmc, kv_rank + 2 * LANES, "ckv_proj")

    cos_t, sin_t = tabs
    hb = 4 if MLA_HEADS % 4 == 0 else 1
    bmq = _tile(m, 1024)
    tab = pl.BlockSpec((bmq, LANES), lambda i, j: (i, 0))
    vec = pl.BlockSpec((1, LANES), lambda i, j: (0, 0))
    mla_q = _matmul(cqn, lw["w_q_up"],
                    functools.partial(_epi_q_up, heads=hb, scale=MLA_QK_DIM ** -0.5 * LOG2E),
                    [cos_t, sin_t, lw["gq_nope"], lw["gq_rope"], lw["gq_swap"]],
                    [tab, tab, vec, vec, vec],
                    jax.ShapeDtypeStruct((MLA_HEADS, m, MLA_QK_PAD), BF16),
                    pl.BlockSpec((hb, bmq, MLA_QK_PAD), lambda i, j: (j, i, 0)),
                    bmq, hb * 3 * LANES, "mla_q_up")
    bmk = _tile(m, 1024)
    tab = pl.BlockSpec((bmk, LANES), lambda i, j: (i, 0))
    mla_k, mla_v = _matmul(ckvn, lw["w_kv_up"], functools.partial(_epi_kv_up, heads=hb),
                           [kr, cos_t, sin_t, lw["gk_nope"], lw["gk_rope"], lw["gk_swap"]],
                           [pl.BlockSpec((bmk, 2 * LANES), lambda i, j: (i, 0)), tab, tab, vec, vec, vec],
                           (jax.ShapeDtypeStruct((MLA_HEADS, m, MLA_QK_PAD), BF16),
                            jax.ShapeDtypeStruct((MLA_HEADS, VT_ROWS, m), BF16)),
                           (pl.BlockSpec((hb, bmk, MLA_QK_PAD), lambda i, j: (j, i, 0)),
                            pl.BlockSpec((hb, VT_ROWS, bmk), lambda i, j: (j, 0, i))),
                           bmk, hb * 2 * LANES, "mla_kv_up")
    return na_qk, na_v, mla_q, mla_k, mla_v


def _dot_nt(a, b):
    return lax.dot_general(a, b, (((1,), (1,)), ((), ())), preferred_element_type=F32)


def _softmax_pv(s_t, vt_blk, state):
    blk_max = jnp.max(s_t, axis=0, keepdims=True)
    m_new = blk_max if state is None else jnp.maximum(state[0], blk_max)
    p_t = jnp.exp2(s_t - m_new).astype(BF16)
    upd = jnp.dot(vt_blk, p_t, preferred_element_type=F32)
    if state is None:
        return m_new, upd
    return m_new, jnp.exp2(state[0] - m_new) * state[1] + upd


def _flash_kernel(*refs, tq, tk, n_chunks, has_ctx):
    if has_ctx:
        q_ref, k_ref, vt_ref, kc_ref, vct_ref, o_ref, s_scr = refs
    else:
        q_ref, k_ref, vt_ref, o_ref, s_scr = refs
    n_sub = q_ref.shape[0] // tq
    sub_rows = [slice(i * tq, (i + 1) * tq) for i in range(n_sub)]
    qs = [q_ref[r, :] for r in sub_rows]

    def chunk(c):
        return pl.ds(pl.multiple_of(c * tk, tk), tk)

    def scores(c, slot):
        k_blk = k_ref[chunk(c), :]
        for i, q in enumerate(qs):
            s_scr[slot, i, :, 0:tq] = _dot_nt(k_blk, q)

    def consume(c, slot, state):
        vt_blk = vt_ref[:, chunk(c)]
        return tuple(_softmax_pv(s_scr[slot, i, :, 0:tq], vt_blk, st) for i, st in enumerate(state))

    first = 0 if has_ctx else 1
    n_pairs = (n_chunks - first) // 2
    if has_ctx:
        init_k, init_vt = kc_ref[...], vct_ref[...]
    else:
        init_k, init_vt = k_ref[0:tk, :], vt_ref[:, 0:tk]
    init_scores = [_dot_nt(init_k, q) for q in qs]
    if n_pairs > 0:
        scores(first, 0)
    state = tuple(_softmax_pv(s, init_vt, None) for s in init_scores)

    if n_pairs > 0:

        def pair(j, state, prefetch):
            c0 = first + 2 * j
            scores(c0 + 1, 1)
            state = consume(c0, 0, state)
            if prefetch:
                scores(c0 + 2, 0)
            return consume(c0 + 1, 1, state)

        state = lax.fori_loop(0, n_pairs - 1, lambda j, st: pair(j, st, True), state)
        state = pair(n_pairs - 1, state, False)
    if (n_chunks - first) % 2:
        last = slice((n_chunks - 1) * tk, n_chunks * tk)
        state = tuple(_softmax_pv(_dot_nt(k_ref[last, :], q), vt_ref[:, last], st) for q, st in zip(qs, state))

    for rows, (_, acc) in zip(sub_rows, state):
        o_t = acc[0:MLA_V_DIM, :] / acc[MLA_V_DIM:MLA_V_DIM + 1, :]
        o_ref[rows, :] = o_t.T.astype(o_ref.dtype)


def _flash(q, k, vt, ctx_kv, *, name):
    heads, mq, dq = q.shape
    mk = k.shape[1]
    bq = _tile(mq, 512)
    tq = _tile(bq, 512)
    tk = _tile(mk, 2048, LANES)
    q_spec = pl.BlockSpec((None, bq, dq), lambda h, i: (h, i, 0))
    whole = lambda a: pl.BlockSpec((None,) + a.shape[1:], lambda h, i: (h, 0, 0))
    in_specs = [q_spec, whole(k), whole(vt)]
    args = [q, k, vt]
    if ctx_kv is not None:
        in_specs += [whole(ctx_kv[0]), whole(ctx_kv[1])]
        args += list(ctx_kv)
    return pl.pallas_call(
        functools.partial(_flash_kernel, tq=tq, tk=tk, n_chunks=mk // tk, has_ctx=ctx_kv is not None),
        grid=(heads, mq // bq),
        in_specs=in_specs,
        out_specs=pl.BlockSpec((bq, MLA_V_DIM), lambda h, i: (i, h)),
        out_shape=jax.ShapeDtypeStruct((mq, heads * MLA_V_DIM), BF16),
        scratch_shapes=[pltpu.VMEM((2, bq // tq, tk, tq + LANES), F32)],
        compiler_params=_params("parallel", "parallel"),
        name=name,
    )(*args)


def _na_kernel(q_ref, k_ref, v_ref, kc_ref, vc_ref, b_ref, o_ref):
    head = lambda h: slice(h * NA_HEAD_DIM, (h + 1) * NA_HEAD_DIM)

    def group_scores(g):
        return [_dot_nt(q_ref[:, head(h)], jnp.concatenate([k_ref[:, head(h)], kc_ref[:, head(h)]], axis=0))
                + b_ref[0, h] for h in range(g * NA_GROUP, (g + 1) * NA_GROUP)]

    def group_softmax(scores):
        probs, inv_l = [], []
        for s in scores:
            p = jnp.exp2(s - jnp.max(s, axis=-1, keepdims=True))
            inv_l.append(1.0 / jnp.sum(p, axis=-1, keepdims=True))
            probs.append(p.astype(BF16))
        return probs, inv_l

    def group_values(g, probs, inv_l):
        out = []
        for i, h in enumerate(range(g * NA_GROUP, (g + 1) * NA_GROUP)):
            v_all = jnp.concatenate([v_ref[:, head(h)], vc_ref[:, head(h)]], axis=0)
            out.append(jnp.dot(probs[i], v_all, preferred_element_type=F32) * inv_l[i])
        return out

    n_groups = NA_HEADS // NA_GROUP
    sc = {g: group_scores(g) for g in range(min(2, n_groups))}
    sm = {0: group_softmax(sc.pop(0))}
    outs = []
    for g in range(n_groups):
        if g + 2 < n_groups:
            sc[g + 2] = group_scores(g + 2)
        if g + 1 < n_groups:
            sm[g + 1] = group_softmax(sc.pop(g + 1))
        outs += group_values(g, *sm.pop(g))
    o_ref[...] = jnp.concatenate(outs, axis=-1).astype(o_ref.dtype)


class _NaPlan(NamedTuple):
    n_blocks: int
    key_rows: int
    n_lo: int
    n_hi: int
    variants: tuple


def _na_window_start(r, rows_n):
    return min(max(r - NA_WIN_H // 2, 0), rows_n - NA_WIN_H)


def _na_slab_start(r0, rows_n, key_rows):
    return min(max(r0 - NA_WIN_H // 2, 0), rows_n - key_rows)


def _na_plan(rows_n):
    key_rows = NA_ROWS + NA_WIN_H - 1
    assert rows_n % NA_ROWS == 0 and rows_n >= key_rows
    n_blocks = rows_n // NA_ROWS

    def shape_of(j):
        r0 = j * NA_ROWS
        base = _na_slab_start(r0, rows_n, key_rows)
        return (base - r0,) + tuple(_na_window_start(r0 + a, rows_n) - base for a in range(NA_ROWS))

    shapes = [shape_of(j) for j in range(n_blocks)]
    interior = (-(NA_WIN_H // 2),) + tuple(range(NA_ROWS))
    inner = [j for j, s in enumerate(shapes) if s == interior]
    assert inner and inner == list(range(inner[0], inner[-1] + 1))
    n_lo, n_hi = inner[0], n_blocks - 1 - inner[-1]
    blocks = list(range(n_lo)) + [inner[0]] + list(range(n_blocks - n_hi, n_blocks))
    variants = tuple((j * NA_ROWS, _na_slab_start(j * NA_ROWS, rows_n, key_rows)) for j in blocks)
    return _NaPlan(n_blocks, key_rows, n_lo, n_hi, variants)


def _na_bias_table(rpb, rows_n, n_ctx):
    plan = _na_plan(rows_n)
    qc = np.arange(GRID_W)
    kc = np.arange(GRID_W)
    cs = np.clip(qc - NA_WIN_W // 2, 0, GRID_W - NA_WIN_W)
    col_ok = (kc[None, :] >= cs[:, None]) & (kc[None, :] < cs[:, None] + NA_WIN_W)
    col_idx = np.clip(kc[None, :] - qc[:, None] + NA_WIN_W - 1, 0, 2 * NA_WIN_W - 2)
    cols = rpb.astype(F32)[:, :, col_idx]
    cols = jnp.where(jnp.asarray(col_ok)[None, None], cols * LOG2E, NEG_INF)
    tables = []
    for r0, base in plan.variants:
        q_row = r0 + np.arange(NA_ROWS)[:, None]
        k_row = base + np.arange(plan.key_rows)[None, :]
        st = np.clip(q_row - NA_WIN_H // 2, 0, rows_n - NA_WIN_H)
        row_ok = (k_row >= st) & (k_row < st + NA_WIN_H)
        d_row = np.clip(k_row - q_row + NA_WIN_H - 1, 0, 2 * NA_WIN_H - 2)
        t = jnp.where(jnp.asarray(row_ok)[None, :, :, None, None], cols[:, d_row], NEG_INF)
        tables.append(t.transpose(0, 1, 3, 2, 4).reshape(NA_HEADS, NA_ROWS * GRID_W, plan.key_rows * GRID_W))
    table = jnp.stack(tables)
    return jnp.concatenate([table, jnp.zeros(table.shape[:3] + (n_ctx,), F32)], axis=-1)


def _na_attention(na_qk, na_v, ctx_qk, ctx_v, bias):
    m = na_qk.shape[0]
    naw = NA_HEADS * NA_HEAD_DIM
    rows_n = m // GRID_W
    plan = _na_plan(rows_n)
    n_ctx = ctx_qk.shape[0]
    q_tok = NA_ROWS * GRID_W
    k_tok = plan.key_rows * GRID_W

    def slab(j):
        return jnp.clip(j * NA_ROWS - NA_WIN_H // 2, 0, rows_n - plan.key_rows) * GRID_W

    def variant(j):
        hi = j - (plan.n_blocks - plan.n_hi)
        return jnp.where(j < plan.n_lo, j, jnp.where(hi >= 0, plan.n_lo + 1 + hi, plan.n_lo))

    E = pl.Element
    return pl.pallas_call(
        _na_kernel,
        grid=(plan.n_blocks,),
        in_specs=[
            pl.BlockSpec((q_tok, naw), lambda j: (j, 0)),
            pl.BlockSpec((E(k_tok), E(naw)), lambda j: (slab(j), naw)),
            pl.BlockSpec((E(k_tok), E(naw)), lambda j: (slab(j), 0)),
            pl.BlockSpec((n_ctx, naw), lambda j: (0, 1)),
            pl.BlockSpec((n_ctx, naw), lambda j: (0, 0)),
            pl.BlockSpec((1, NA_HEADS, q_tok, k_tok + n_ctx), lambda j: (variant(j), 0, 0, 0)),
        ],
        out_specs=pl.BlockSpec((q_tok, naw), lambda j: (j, 0)),
        out_shape=jax.ShapeDtypeStruct((m, naw), BF16),
        compiler_params=_params("parallel"),
        name="na_attention",
    )(na_qk, na_qk, na_v, ctx_qk, ctx_v, bias)


def _merge_kernel(h_ref, yn_ref, ym_ref, wgn_ref, wgm_ref, wbn_ref, wbm_ref, o_ref):
    h = h_ref[...]
    gn = jax.nn.sigmoid(jnp.dot(h, wgn_ref[...], preferred_element_type=F32))
    gm = jax.nn.sigmoid(jnp.dot(h, wgm_ref[...], preferred_element_type=F32))
    a = jnp.dot(yn_ref[...], wbn_ref[...], preferred_element_type=F32)
    b = jnp.dot(ym_ref[...], wbm_ref[...], preferred_element_type=F32)
    o_ref[...] = (gn * a + gm * b).astype(o_ref.dtype)


def _merge(h, y_na, y_mla, lw):
    m, d = h.shape
    bm = _tile(m, 512)
    bn = _tile(d, 512, LANES)
    nj = d // bn
    kn, km = y_na.shape[1], y_mla.shape[1]
    return pl.pallas_call(
        _merge_kernel,
        grid=(nj, m // bm),
        in_specs=[pl.BlockSpec((bm, d), lambda j, i: (i, 0)),
                  pl.BlockSpec((bm, kn), lambda j, i: (i, 0)),
                  pl.BlockSpec((bm, km), lambda j, i: (i, 0)),
                  _w_spec(lw["w_gates"], bn, j_first=True),
                  _w_spec(lw["w_gates"], bn, shift=nj, j_first=True),
                  _w_spec(lw["w_b_na"], bn, j_first=True),
                  _w_spec(lw["w_b_mla"], bn, j_first=True)],
        out_specs=pl.BlockSpec((bm, bn), lambda j, i: (i, j)),
        out_shape=jax.ShapeDtypeStruct((m, d), BF16),
        compiler_params=_params("parallel", "parallel"),
        name="branch_merge",
    )(h, y_na, y_mla, lw["w_gates"].arr, lw["w_gates"].arr, lw["w_b_na"].arr, lw["w_b_mla"].arr)


def _residual_matmul(a, w, x, gate, name):
    m, d = x.shape
    bm = _tile(m, 1024)
    bn = _tile(d, 512, LANES)
    return _matmul(a, w, _epi_residual, [x, gate.reshape(1, d)],
                   [pl.BlockSpec((bm, bn), lambda i, j: (i, j)),
                    pl.BlockSpec((1, bn), lambda i, j: (0, j))],
                   jax.ShapeDtypeStruct((m, d), F32),
                   pl.BlockSpec((bm, bn), lambda i, j: (i, j)), bm, bn, name)


def _ffn_up_kernel(x_ref, wg_ref, wv_ref, halo_ref, cw_ref, cb_ref, o_ref):
    x = x_ref[...]
    g = jnp.dot(x, wg_ref[...], preferred_element_type=F32)
    bm = g.shape[0]
    rows = lax.broadcasted_iota(jnp.int32, g.shape, 0)
    g_prev = jnp.where(rows == 0, halo_ref[0:1, :], pltpu.roll(g, 1, axis=0))
    g_next = jnp.where(rows == bm - 1, halo_ref[1:2, :], pltpu.roll(g, bm - 1, axis=0))
    y = g_prev * cw_ref[0:1, :] + g * cw_ref[1:2, :] + g_next * cw_ref[2:3, :] + cb_ref[...]
    v = jnp.dot(x, wv_ref[...], preferred_element_type=F32)
    o_ref[...] = (y * jax.nn.sigmoid(y) * v).astype(o_ref.dtype)


def _ffn_up(h2, lw):
    m, d = h2.shape
    dff = lw["conv_b"].shape[1]
    bm = _tile(m, 1024)
    bn = _tile(dff, 512, LANES)
    nt, nj = m // bm, dff // bn

    edge = h2.reshape(nt, bm, d)[:, (0, bm - 1), :].reshape(2 * nt, d)
    pad = (-edge.shape[0]) % 16
    edge = jnp.pad(edge, ((0, pad), (0, 0)))
    bne = _tile(dff, 1024, LANES)
    eg = _matmul(edge, lw["w_up"]._replace(n=dff), _epi_cast, [], [],
                 jax.ShapeDtypeStruct((edge.shape[0], dff), F32),
                 pl.BlockSpec((edge.shape[0], bne), lambda i, j: (i, j)),
                 edge.shape[0], bne, "ffn_edge_rows")
    eg = eg[:2 * nt].reshape(nt, 2, dff)
    zero = jnp.zeros((1, dff), F32)
    prev_rows = jnp.concatenate([zero, eg[:-1, 1]], axis=0)
    next_rows = jnp.concatenate([eg[1:, 0], zero], axis=0)
    halo = jnp.stack([prev_rows, next_rows], axis=1)

    return pl.pallas_call(
        _ffn_up_kernel,
        grid=(m // bm, nj),
        in_specs=[pl.BlockSpec((bm, d), lambda i, j: (i, 0)),
                  _w_spec(lw["w_up"], bn),
                  _w_spec(lw["w_up"], bn, shift=nj),
                  pl.BlockSpec((None, 2, bn), lambda i, j: (i, 0, j)),
                  pl.BlockSpec((CONV_W, bn), lambda i, j: (0, j)),
                  pl.BlockSpec((1, bn), lambda i, j: (0, j))],
        out_specs=pl.BlockSpec((bm, bn), lambda i, j: (i, j)),
        out_shape=jax.ShapeDtypeStruct((m, dff), BF16),
        compiler_params=_params("parallel", "parallel"),
        name="ffn_up_conv",
    )(h2, lw["w_up"].arr, lw["w_up"].arr, halo, lw["conv_w"], lw["conv_b"])


def _rope_partner():
    half = MLA_ROPE_DIM // 2
    quarter = half // 2
    idx = np.arange(MLA_ROPE_DIM)
    return np.where((idx % half) < quarter, idx + quarter, idx - quarter)


def _pad_lanes(a):
    return jnp.pad(a, [(0, 0)] * (a.ndim - 1) + [(0, LANES - a.shape[-1])])


def _stacked_weights(w_in, mla_w_q_up, mla_w_kv_up, w_branch_na, w_branch_mla, w_out, ffn_w_up, ffn_w_down):
    depth = w_in.shape[0]
    naw = NA_HEADS * NA_HEAD_DIM
    q_rank = mla_w_q_up.shape[1]
    kv_rank = mla_w_kv_up.shape[1]
    perm = _rope_partner()
    o_cq = 3 * naw
    o_ckv = o_cq + q_rank
    o_kr = o_ckv + kv_rank
    o_g = o_kr + MLA_ROPE_DIM
    w_in = w_in.astype(BF16)
    w_kr = w_in[:, :, o_kr:o_g]
    wq = mla_w_q_up.reshape(depth, q_rank, MLA_HEADS, MLA_QK_DIM)
    wq_rope = wq[..., MLA_NOPE_DIM:]
    return {
        "in_head": w_in,
        "ckv_kr": jnp.concatenate([w_in[:, :, o_ckv:o_kr], _pad_lanes(w_kr), _pad_lanes(w_kr[:, :, perm])], axis=2),
        "gates": w_in[:, :, o_g:],
        "q_up": jnp.concatenate([wq[..., :MLA_NOPE_DIM], _pad_lanes(wq_rope), _pad_lanes(wq_rope[..., perm])],
                                axis=3).reshape(depth, q_rank, MLA_HEADS * 3 * LANES).astype(BF16),
        "kv_up": mla_w_kv_up.astype(BF16),
        "b_na": w_branch_na.astype(BF16),
        "b_mla": w_branch_mla.astype(BF16),
        "out": w_out.astype(BF16),
        "up": ffn_w_up.astype(BF16),
        "down": ffn_w_down.astype(BF16),
    }


def _layer_weights(l, sw, na_q_norm, na_k_norm, mla_cq_norm, mla_ckv_norm, mla_q_norm, mla_k_norm,
                   ffn_conv_w, ffn_conv_b):
    naw = NA_HEADS * NA_HEAD_DIM
    q_rank = sw["q_up"].shape[1]
    kv_rank = sw["kv_up"].shape[1]
    perm = _rope_partner()
    whole = lambda a: _W(a, l, 0, a.shape[2])
    lw = {
        "w_na_qk": _W(sw["in_head"], l, 0, 2 * naw),
        "w_na_v": _W(sw["in_head"], l, 2 * naw, naw),
        "w_cq": _W(sw["in_head"], l, 3 * naw, q_rank),
        "w_ckv_kr": whole(sw["ckv_kr"]),
        "w_gates": whole(sw["gates"]),
        "w_q_up": whole(sw["q_up"]),
        "w_kv_up": whole(sw["kv_up"]),
        "w_b_na": whole(sw["b_na"]),
        "w_b_mla": whole(sw["b_mla"]),
        "w_out": whole(sw["out"]),
        "w_up": whole(sw["up"]),
        "w_down": whole(sw["down"]),
        "g_na_qk": jnp.concatenate([jnp.tile(na_q_norm[l] * (NA_HEAD_DIM ** -0.5 * LOG2E), NA_HEADS),
                                    jnp.tile(na_k_norm[l], NA_HEADS)]).reshape(1, 2 * naw),
        "g_cq": mla_cq_norm[l].reshape(1, q_rank),
        "g_ckv": mla_ckv_norm[l].reshape(1, kv_rank),
        "conv_w": ffn_conv_w[l],
        "conv_b": ffn_conv_b[l].reshape(1, -1),
    }
    for tag, g in (("q", mla_q_norm[l]), ("k", mla_k_norm[l])):
        g_rope = g[MLA_NOPE_DIM:]
        lw[f"g{tag}_nope"] = g[:MLA_NOPE_DIM].reshape(1, LANES)
        lw[f"g{tag}_rope"] = _pad_lanes(g_rope).reshape(1, LANES)
        lw[f"g{tag}_swap"] = _pad_lanes(g_rope[perm]).reshape(1, LANES)
    return lw


def _rope_tables(n_tok):
    t = jnp.arange(n_tok, dtype=jnp.int32)
    half = MLA_ROPE_DIM // 2
    inv_freq = ROPE_THETA ** (-jnp.arange(0, half, 2, dtype=F32) / half)
    cs, sn = [], []
    for pos in (t // GRID_W, t % GRID_W):
        ang = pos.astype(F32)[:, None] * inv_freq[None, :]
        cs += [jnp.cos(ang), jnp.cos(ang)]
        sn += [-jnp.sin(ang), jnp.sin(ang)]
    return _pad_lanes(jnp.concatenate(cs, axis=1)), _pad_lanes(jnp.concatenate(sn, axis=1))


def _identity_tables(n_tok):
    cos_t = _pad_lanes(jnp.ones((n_tok, MLA_ROPE_DIM), F32))
    return cos_t, jnp.zeros((n_tok, LANES), F32)


def _head_major(a, heads):
    rows = a.shape[0]
    return a.reshape(rows, heads, -1).transpose(1, 0, 2)


def _head_major_vt(v, heads):
    rows = v.shape[0]
    vt = v.reshape(rows, heads, MLA_V_DIM).transpose(1, 2, 0)
    tail = jnp.zeros((heads, BF16_SUBLANES, rows), v.dtype).at[:, 0, :].set(1.0)
    return jnp.concatenate([vt, tail], axis=1)


def kernel(x, c, ctx, c_ctx, ada_down, ada_up, ada_bias, norm_mix, norm_ffn, w_in, na_q_norm, na_k_norm, na_rpb, mla_cq_norm, mla_ckv_norm, mla_w_q_up, mla_w_kv_up, mla_q_norm, mla_k_norm, w_branch_na, w_branch_mla, w_out, ffn_w_up, ffn_conv_w, ffn_conv_b, ffn_w_down):
    batch, n_tok, d = x.shape
    assert batch == 1 and c.shape[0] == 1
    depth = w_in.shape[0]
    naw = NA_HEADS * NA_HEAD_DIM
    xs = x[0]
    cs = ctx[0]
    lat_tabs = _rope_tables(n_tok)
    ctx_tabs = _identity_tables(cs.shape[0])
    cond = jnp.zeros((8, d), F32).at[0].set(c[0]).at[1].set(c_ctx)
    sw = _stacked_weights(w_in, mla_w_q_up, mla_w_kv_up, w_branch_na, w_branch_mla, w_out, ffn_w_up, ffn_w_down)

    for l in range(depth):
        last = l == depth - 1
        lw = _layer_weights(l, sw, na_q_norm, na_k_norm, mla_cq_norm, mla_ckv_norm, mla_q_norm, mla_k_norm,
                            ffn_conv_w, ffn_conv_b)
        mods = _adaln(cond, ada_down, ada_up, ada_bias, l)
        sh1, sc1, g1, sh2, sc2, g2 = [mods[0, i * d:(i + 1) * d] for i in range(N_MOD)]
        csh1, csc1, cg1, csh2, csc2, cg2 = [mods[1, i * d:(i + 1) * d] for i in range(N_MOD)]

        h = _norm_mod(xs, norm_mix[l], sc1, sh1)
        hc = _norm_mod(cs, norm_mix[l], csc1, csh1)
        na_qk, na_v, mq, mk, mv = _project(h, lw, lat_tabs)
        c_qk, c_v, cmq, cmk, cmv = _project(hc, lw, ctx_tabs)

        y_na = _na_attention(na_qk, na_v, c_qk, c_v, _na_bias_table(na_rpb[l], n_tok // GRID_W, cs.shape[0]))
        y_mla = _flash(mq, mk, mv, (cmk, cmv), name="mla_attention")
        u = _merge(h, y_na, y_mla, lw)
        xs = _residual_matmul(u, lw["w_out"], xs, g1, "out_proj")
        h2 = _norm_mod(xs, norm_ffn[l], sc2, sh2)
        xs = _residual_matmul(_ffn_up(h2, lw), lw["w_down"], xs, g2, "ffn_down")

        if not last:
            yc_na = _flash(_head_major(c_qk[:, :naw], NA_HEADS), _head_major(c_qk[:, naw:], NA_HEADS),
                           _head_major_vt(c_v, NA_HEADS), None, name="ctx_na_attention")
            yc_mla = _flash(cmq, cmk, cmv, None, name="ctx_mla_attention")
            uc = _merge(hc, yc_na, yc_mla, lw)
            cs = _residual_matmul(uc, lw["w_out"], cs, cg1, "ctx_out_proj")
            hc2 = _norm_mod(cs, norm_ffn[l], csc2, csh2)
            cs = _residual_matmul(_ffn_up(hc2, lw), lw["w_down"], cs, cg2, "ctx_ffn_down")
    return xs[None]
```

```python
import functools
from typing import NamedTuple

import numpy as np
import jax
import jax.numpy as jnp
from jax import lax
from jax.experimental import pallas as pl
from jax.experimental.pallas import tpu as pltpu

F32 = jnp.float32
BF16 = jnp.bfloat16

GRID_W = 64
EPS = 1e-6
NEG_INF = -1e30
N_MOD = 6

NA_HEADS = 16
NA_HEAD_DIM = 128
NA_WIN_H = 8
NA_WIN_W = 16
NA_GROUP = 1
NA_ROWS = 2

MLA_HEADS = 16
MLA_NOPE_DIM = 128
MLA_ROPE_DIM = 64
MLA_QK_DIM = MLA_NOPE_DIM + MLA_ROPE_DIM
MLA_V_DIM = 128
ROPE_THETA = 10000.0
CONV_W = 3

LANES = 128
MLA_QK_PAD = 2 * LANES
BF16_SUBLANES = 16
VT_ROWS = MLA_V_DIM + BF16_SUBLANES
LOG2E = 1.4426950408889634
VMEM_LIMIT_BYTES = 56 * 1024 * 1024


def _tile(dim, pref, mult=8):
    if dim <= pref:
        return dim
    t = (pref // mult) * mult
    while t >= mult:
        if dim % t == 0:
            return t
        t -= mult
    return dim


def _params(*sem):
    return pltpu.CompilerParams(dimension_semantics=sem, vmem_limit_bytes=VMEM_LIMIT_BYTES)


def _ada_down_kernel(c_ref, w_ref, o_ref):
    c = c_ref[...]
    a = (c * jax.nn.sigmoid(c)).astype(BF16)
    o_ref[...] = jnp.dot(a, w_ref[...].astype(BF16), preferred_element_type=F32)


def _ada_up_kernel(t_ref, w_ref, b_ref, o_ref):
    o_ref[...] = jnp.dot(t_ref[...].astype(BF16), w_ref[...].astype(BF16),
                         preferred_element_type=F32) + b_ref[...]


def _adaln(cond, down, up, bias, layer):
    rows, d = cond.shape
    rank = down.shape[2]
    n = up.shape[2]
    bn1 = _tile(rank, 512, LANES)
    t = pl.pallas_call(
        _ada_down_kernel,
        grid=(rank // bn1,),
        in_specs=[pl.BlockSpec((rows, d), lambda j: (0, 0)),
                  pl.BlockSpec((None, d, bn1), lambda j: (layer, 0, j))],
        out_specs=pl.BlockSpec((rows, bn1), lambda j: (0, j)),
        out_shape=jax.ShapeDtypeStruct((rows, rank), F32),
        compiler_params=_params("parallel"),
        name="ada_down",
    )(cond, down)
    bn2 = _tile(n, 4096, LANES)
    return pl.pallas_call(
        _ada_up_kernel,
        grid=(n // bn2,),
        in_specs=[pl.BlockSpec((rows, rank), lambda j: (0, 0)),
                  pl.BlockSpec((None, rank, bn2), lambda j: (layer, 0, j)),
                  pl.BlockSpec((None, 1, bn2), lambda j: (layer, 0, j))],
        out_specs=pl.BlockSpec((rows, bn2), lambda j: (0, j)),
        out_shape=jax.ShapeDtypeStruct((rows, n), F32),
        compiler_params=_params("parallel"),
        name="ada_up",
    )(t, up, bias.reshape(bias.shape[0], 1, n))


def _norm_mod_kernel(x_ref, g_ref, sc_ref, sh_ref, o_ref):
    x = x_ref[...]
    y = x * lax.rsqrt(jnp.mean(x * x, axis=-1, keepdims=True) + EPS) * g_ref[...]
    o_ref[...] = (y * (1.0 + sc_ref[...]) + sh_ref[...]).astype(o_ref.dtype)


def _norm_mod(x, g, sc, sh):
    m, d = x.shape
    bm = _tile(m, 512)
    vec = pl.BlockSpec((1, d), lambda i: (0, 0))
    return pl.pallas_call(
        _norm_mod_kernel,
        grid=(m // bm,),
        in_specs=[pl.BlockSpec((bm, d), lambda i: (i, 0)), vec, vec, vec],
        out_specs=pl.BlockSpec((bm, d), lambda i: (i, 0)),
        out_shape=jax.ShapeDtypeStruct((m, d), BF16),
        compiler_params=_params("parallel"),
        name="norm_mod",
    )(x, g.reshape(1, d), sc.reshape(1, d), sh.reshape(1, d))


def _mm_kernel(x_ref, w_ref, *rest, epilogue):
    acc = jnp.dot(x_ref[...], w_ref[...], preferred_element_type=F32)
    epilogue(acc, *rest)


class _W(NamedTuple):
    arr: jax.Array
    layer: int
    col0: int
    n: int


def _w_spec(w, bn, shift=0, j_first=False):
    blk0, rem = divmod(w.col0, bn)
    assert rem == 0 and w.n % bn == 0
    if j_first:
        return pl.BlockSpec((None, w.arr.shape[1], bn), lambda j, i: (w.layer, 0, blk0 + shift + j))
    return pl.BlockSpec((None, w.arr.shape[1], bn), lambda i, j: (w.layer, 0, blk0 + shift + j))


def _matmul(x, w, epilogue, extras, extra_specs, out_shapes, out_specs, bm, bn, name):
    m, k = x.shape
    return pl.pallas_call(
        functools.partial(_mm_kernel, epilogue=epilogue),
        grid=(m // bm, w.n // bn),
        in_specs=[pl.BlockSpec((bm, k), lambda i, j: (i, 0)), _w_spec(w, bn)] + list(extra_specs),
        out_specs=out_specs,
        out_shape=out_shapes,
        compiler_params=_params("parallel", "parallel"),
        name=name,
    )(x, w.arr, *extras)


def _epi_cast(acc, o_ref):
    o_ref[...] = acc.astype(o_ref.dtype)


def _epi_headnorm(acc, g_ref, o_ref):
    for c in range(acc.shape[1] // LANES):
        sl = slice(c * LANES, (c + 1) * LANES)
        a = acc[:, sl]
        r = lax.rsqrt(jnp.mean(a * a, axis=-1, keepdims=True) + EPS)
        o_ref[:, sl] = (a * r * g_ref[:, sl]).astype(o_ref.dtype)


def _epi_rownorm(acc, g_ref, o_ref):
    r = lax.rsqrt(jnp.mean(acc * acc, axis=-1, keepdims=True) + EPS)
    o_ref[...] = (acc * r * g_ref[...]).astype(o_ref.dtype)


def _epi_ckv(acc, g_ref, ckv_ref, kr_ref, *, kv_rank):
    a = acc[:, :kv_rank]
    r = lax.rsqrt(jnp.mean(a * a, axis=-1, keepdims=True) + EPS)
    ckv_ref[...] = (a * r * g_ref[...]).astype(ckv_ref.dtype)
    kr_ref[...] = acc[:, kv_rank:]


def _epi_q_up(acc, c_ref, s_ref, gn_ref, gr_ref, gs_ref, o_ref, *, heads, scale):
    cg = c_ref[...] * gr_ref[...]
    sg = s_ref[...] * gs_ref[...]
    for h in range(heads):
        base = h * 3 * LANES
        nope = acc[:, base:base + LANES]
        rp = acc[:, base + LANES:base + 2 * LANES]
        sw = acc[:, base + 2 * LANES:base + 3 * LANES]
        ss = jnp.sum(nope * nope, axis=-1, keepdims=True) + jnp.sum(rp * rp, axis=-1, keepdims=True)
        r = lax.rsqrt(ss * (1.0 / MLA_QK_DIM) + EPS) * scale
        o_ref[h, :, 0:LANES] = (nope * r * gn_ref[...]).astype(o_ref.dtype)
        o_ref[h, :, LANES:2 * LANES] = ((rp * cg + sw * sg) * r).astype(o_ref.dtype)


def _ones_row_tile(cols, dtype):
    rows = lax.broadcasted_iota(jnp.int32, (BF16_SUBLANES, cols), 0)
    return jnp.where(rows == 0, 1.0, 0.0).astype(dtype)


def _epi_kv_up(acc, kr_ref, c_ref, s_ref, gn_ref, gr_ref, gs_ref, k_ref, vt_ref, *, heads):
    kr = kr_ref[:, 0:LANES]
    krs = kr_ref[:, LANES:2 * LANES]
    ssr = jnp.sum(kr * kr, axis=-1, keepdims=True)
    rope = kr * (c_ref[...] * gr_ref[...]) + krs * (s_ref[...] * gs_ref[...])
    for h in range(heads):
        base = h * 2 * LANES
        nope = acc[:, base:base + LANES]
        ss = jnp.sum(nope * nope, axis=-1, keepdims=True) + ssr
        r = lax.rsqrt(ss * (1.0 / MLA_QK_DIM) + EPS)
        k_ref[h, :, 0:LANES] = (nope * r * gn_ref[...]).astype(k_ref.dtype)
        k_ref[h, :, LANES:2 * LANES] = (rope * r).astype(k_ref.dtype)
        vt_ref[h, 0:MLA_V_DIM, :] = acc[:, base + LANES:base + 2 * LANES].T.astype(vt_ref.dtype)
        vt_ref[h, MLA_V_DIM:VT_ROWS, :] = _ones_row_tile(acc.shape[0], vt_ref.dtype)


def _epi_residual(acc, x_ref, g_ref, o_ref):
    o_ref[...] = x_ref[...] + g_ref[...] * acc


def _project(h, lw, tabs):
    m, d = h.shape
    bm = _tile(m, 1024)
    naw = NA_HEADS * NA_HEAD_DIM
    row = lambda n: pl.BlockSpec((1, n), lambda i, j: (0, 0))

    bn = _tile(2 * naw, 1024, LANES)
    na_qk = _matmul(h, lw["w_na_qk"], _epi_headnorm, [lw["g_na_qk"]],
                    [pl.BlockSpec((1, bn), lambda i, j: (0, j))],
                    jax.ShapeDtypeStruct((m, 2 * naw), BF16),
                    pl.BlockSpec((bm, bn), lambda i, j: (i, j)), bm, bn, "na_qk_proj")
    bn = _tile(naw, 1024, LANES)
    na_v = _matmul(h, lw["w_na_v"], _epi_cast, [], [],
                   jax.ShapeDtypeStruct((m, naw), BF16),
                   pl.BlockSpec((bm, bn), lambda i, j: (i, j)), bm, bn, "na_v_proj")

    q_rank = lw["w_cq"].n
    kv_rank = lw["w_ckv_kr"].n - 2 * LANES
    bmc = _tile(m, 512)
    cqn = _matmul(h, lw["w_cq"], _epi_rownorm, [lw["g_cq"]], [row(q_rank)],
                  jax.ShapeDtypeStruct((m, q_rank), BF16),
                  pl.BlockSpec((bmc, q_rank), lambda i, j: (i, 0)), bmc, q_rank, "cq_proj")
    ckvn, kr = _matmul(h, lw["w_ckv_kr"], functools.partial(_epi_ckv, kv_rank=kv_rank),
                       [lw["g_ckv"]], [row(kv_rank)],
                       (jax.ShapeDtypeStruct((m, kv_rank), BF16),
                        jax.ShapeDtypeStruct((m, 2 * LANES), F32)),
                       (pl.BlockSpec((bmc, kv_rank), lambda i, j: (i, 0)),
                        pl.BlockSpec((bmc, 2 * LANES), lambda i, j: (i, 0))),
                       bmc, kv_rank + 2 * LANES, "ckv_proj")

    cos_t, sin_t = tabs
    hb = 4 if MLA_HEADS % 4 == 0 else 1
    bmq = _tile(m, 1024)
    tab = pl.BlockSpec((bmq, LANES), lambda i, j: (i, 0))
    vec = pl.BlockSpec((1, LANES), lambda i, j: (0, 0))
    mla_q = _matmul(cqn, lw["w_q_up"],
                    functools.partial(_epi_q_up, heads=hb, scale=MLA_QK_DIM ** -0.5 * LOG2E),
                    [cos_t, sin_t, lw["gq_nope"], lw["gq_rope"], lw["gq_swap"]],
                    [tab, tab, vec, vec, vec],
                    jax.ShapeDtypeStruct((MLA_HEADS, m, MLA_QK_PAD), BF16),
                    pl.BlockSpec((hb, bmq, MLA_QK_PAD), lambda i, j: (j, i, 0)),
                    bmq, hb * 3 * LANES, "mla_q_up")
    bmk = _tile(m, 1024)
    tab = pl.BlockSpec((bmk, LANES), lambda i, j: (i, 0))
    mla_k, mla_v = _matmul(ckvn, lw["w_kv_up"], functools.partial(_epi_kv_up, heads=hb),
                           [kr, cos_t, sin_t, lw["gk_nope"], lw["gk_rope"], lw["gk_swap"]],
                           [pl.BlockSpec((bmk, 2 * LANES), lambda i, j: (i, 0)), tab, tab, vec, vec, vec],
                           (jax.ShapeDtypeStruct((MLA_HEADS, m, MLA_QK_PAD), BF16),
                            jax.ShapeDtypeStruct((MLA_HEADS, VT_ROWS, m), BF16)),
                           (pl.BlockSpec((hb, bmk, MLA_QK_PAD), lambda i, j: (j, i, 0)),
                            pl.BlockSpec((hb, VT_ROWS, bmk), lambda i, j: (j, 0, i))),
                           bmk, hb * 2 * LANES, "mla_kv_up")
    return na_qk, na_v, mla_q, mla_k, mla_v


def _dot_nt(a, b):
    return lax.dot_general(a, b, (((1,), (1,)), ((), ())), preferred_element_type=F32)


def _softmax_pv(s_t, vt_blk, state):
    blk_max = jnp.max(s_t, axis=0, keepdims=True)
    m_new = blk_max if state is None else jnp.maximum(state[0], blk_max)
    p_t = jnp.exp2(s_t - m_new).astype(BF16)
    upd = jnp.dot(vt_blk, p_t, preferred_element_type=F32)
    if state is None:
        return m_new, upd
    return m_new, jnp.exp2(state[0] - m_new) * state[1] + upd


def _flash_kernel(*refs, tq, tk, n_chunks, has_ctx):
    if has_ctx:
        q_ref, k_ref, vt_ref, kc_ref, vct_ref, o_ref, s_scr = refs
    else:
        q_ref, k_ref, vt_ref, o_ref, s_scr = refs
    n_sub = q_ref.shape[0] // tq
    sub_rows = [slice(i * tq, (i + 1) * tq) for i in range(n_sub)]
    qs = [q_ref[r, :] for r in sub_rows]

    def chunk(c):
        return pl.ds(pl.multiple_of(c * tk, tk), tk)

    def scores(c, slot):
        k_blk = k_ref[chunk(c), :]
        for i, q in enumerate(qs):
            s_scr[slot, i, :, 0:tq] = _dot_nt(k_blk, q)

    def consume(c, slot, state):
        vt_blk = vt_ref[:, chunk(c)]
        return tuple(_softmax_pv(s_scr[slot, i, :, 0:tq], vt_blk, st) for i, st in enumerate(state))

    first = 0 if has_ctx else 1
    n_pairs = (n_chunks - first) // 2
    if has_ctx:
        init_k, init_vt = kc_ref[...], vct_ref[...]
    else:
        init_k, init_vt = k_ref[0:tk, :], vt_ref[:, 0:tk]
    init_scores = [_dot_nt(init_k, q) for q in qs]
    if n_pairs > 0:
        scores(first, 0)
    state = tuple(_softmax_pv(s, init_vt, None) for s in init_scores)

    if n_pairs > 0:

        def pair(j, state, prefetch):
            c0 = first + 2 * j
            scores(c0 + 1, 1)
            state = consume(c0, 0, state)
            if prefetch:
                scores(c0 + 2, 0)
            return consume(c0 + 1, 1, state)

        state = lax.fori_loop(0, n_pairs - 1, lambda j, st: pair(j, st, True), state)
        state = pair(n_pairs - 1, state, False)
    if (n_chunks - first) % 2:
        last = slice((n_chunks - 1) * tk, n_chunks * tk)
        state = tuple(_softmax_pv(_dot_nt(k_ref[last, :], q), vt_ref[:, last], st) for q, st in zip(qs, state))

    for rows, (_, acc) in zip(sub_rows, state):
        o_t = acc[0:MLA_V_DIM, :] / acc[MLA_V_DIM:MLA_V_DIM + 1, :]
        o_ref[rows, :] = o_t.T.astype(o_ref.dtype)


def _flash(q, k, vt, ctx_kv, *, name):
    heads, mq, dq = q.shape
    mk = k.shape[1]
    bq = _tile(mq, 512)
    tq = _tile(bq, 512)
    tk = _tile(mk, 2048, LANES)
    q_spec = pl.BlockSpec((None, bq, dq), lambda h, i: (h, i, 0))
    whole = lambda a: pl.BlockSpec((None,) + a.shape[1:], lambda h, i: (h, 0, 0))
    in_specs = [q_spec, whole(k), whole(vt)]
    args = [q, k, vt]
    if ctx_kv is not None:
        in_specs += [whole(ctx_kv[0]), whole(ctx_kv[1])]
        args += list(ctx_kv)
    return pl.pallas_call(
        functools.partial(_flash_kernel, tq=tq, tk=tk, n_chunks=mk // tk, has_ctx=ctx_kv is not None),
        grid=(heads, mq // bq),
        in_specs=in_specs,
        out_specs=pl.BlockSpec((bq, MLA_V_DIM), lambda h, i: (i, h)),
        out_shape=jax.ShapeDtypeStruct((mq, heads * MLA_V_DIM), BF16),
        scratch_shapes=[pltpu.VMEM((2, bq // tq, tk, tq + LANES), F32)],
        compiler_params=_params("parallel", "parallel"),
        name=name,
    )(*args)


def _na_kernel(q_ref, k_ref, v_ref, kc_ref, vc_ref, b_ref, o_ref):
    head = lambda h: slice(h * NA_HEAD_DIM, (h + 1) * NA_HEAD_DIM)

    def group_scores(g):
        return [_dot_nt(q_ref[:, head(h)], jnp.concatenate([kc_ref[:, head(h)], k_ref[:, head(h)]], axis=0))
                + b_ref[0, h] for h in range(g * NA_GROUP, (g + 1) * NA_GROUP)]

    def group_softmax(scores):
        probs, inv_l = [], []
        for s in scores:
            p = jnp.exp2(s - jnp.max(s, axis=-1, keepdims=True))
            inv_l.append(1.0 / jnp.sum(p, axis=-1, keepdims=True))
            probs.append(p.astype(BF16))
        return probs, inv_l

    def group_values(g, probs, inv_l):
        out = []
        for i, h in enumerate(range(g * NA_GROUP, (g + 1) * NA_GROUP)):
            v_all = jnp.concatenate([vc_ref[:, head(h)], v_ref[:, head(h)]], axis=0)
            out.append(jnp.dot(probs[i], v_all, preferred_element_type=F32) * inv_l[i])
        return out

    n_groups = NA_HEADS // NA_GROUP
    sc = {g: group_scores(g) for g in range(min(2, n_groups))}
    sm = {0: group_softmax(sc.pop(0))}
    outs = []
    for g in range(n_groups):
        if g + 2 < n_groups:
            sc[g + 2] = group_scores(g + 2)
        if g + 1 < n_groups:
            sm[g + 1] = group_softmax(sc.pop(g + 1))
        outs += group_values(g, *sm.pop(g))
    o_ref[...] = jnp.concatenate(outs, axis=-1).astype(o_ref.dtype)


class _NaPlan(NamedTuple):
    n_blocks: int
    key_rows: int
    n_lo: int
    n_hi: int
    variants: tuple


def _na_window_start(r, rows_n):
    return min(max(r - NA_WIN_H // 2, 0), rows_n - NA_WIN_H)


def _na_slab_start(r0, rows_n, key_rows):
    return min(max(r0 - NA_WIN_H // 2, 0), rows_n - key_rows)


def _na_plan(rows_n):
    key_rows = NA_ROWS + NA_WIN_H - 1
    assert rows_n % NA_ROWS == 0 and rows_n >= key_rows
    n_blocks = rows_n // NA_ROWS

    def shape_of(j):
        r0 = j * NA_ROWS
        base = _na_slab_start(r0, rows_n, key_rows)
        return (base - r0,) + tuple(_na_window_start(r0 + a, rows_n) - base for a in range(NA_ROWS))

    shapes = [shape_of(j) for j in range(n_blocks)]
    interior = (-(NA_WIN_H // 2),) + tuple(range(NA_ROWS))
    inner = [j for j, s in enumerate(shapes) if s == interior]
    assert inner and inner == list(range(inner[0], inner[-1] + 1))
    n_lo, n_hi = inner[0], n_blocks - 1 - inner[-1]
    blocks = list(range(n_lo)) + [inner[0]] + list(range(n_blocks - n_hi, n_blocks))
    variants = tuple((j * NA_ROWS, _na_slab_start(j * NA_ROWS, rows_n, key_rows)) for j in blocks)
    return _NaPlan(n_blocks, key_rows, n_lo, n_hi, variants)


def _na_bias_table(rpb, rows_n, n_ctx):
    plan = _na_plan(rows_n)
    qc = np.arange(GRID_W)
    kc = np.arange(GRID_W)
    cs = np.clip(qc - NA_WIN_W // 2, 0, GRID_W - NA_WIN_W)
    col_ok = (kc[None, :] >= cs[:, None]) & (kc[None, :] < cs[:, None] + NA_WIN_W)
    col_idx = np.clip(kc[None, :] - qc[:, None] + NA_WIN_W - 1, 0, 2 * NA_WIN_W - 2)
    cols = rpb.astype(F32)[:, :, col_idx]
    cols = jnp.where(jnp.asarray(col_ok)[None, None], cols * LOG2E, NEG_INF)
    kr = plan.key_rows
    cols = jnp.pad(cols, ((0, 0), (kr, kr), (0, 0), (0, 0)), constant_values=NEG_INF)
    tables = []
    for r0, base in plan.variants:
        parts = []
        for a in range(NA_ROWS):
            q_row = r0 + a
            k_row = base + np.arange(kr)
            st = _na_window_start(q_row, rows_n)
            row_ok = (k_row >= st) & (k_row < st + NA_WIN_H)
            d0 = base - q_row + NA_WIN_H - 1 + kr
            run = jnp.where(jnp.asarray(row_ok)[None, :, None, None], cols[:, d0:d0 + kr], NEG_INF)
            parts.append(run.transpose(0, 2, 1, 3).reshape(NA_HEADS, GRID_W, kr * GRID_W))
        tables.append(jnp.concatenate(parts, axis=1))
    table = jnp.stack(tables)
    return jnp.pad(table, ((0, 0), (0, 0), (0, 0), (n_ctx, 0)))


def _na_attention(na_qk, na_v, ctx_qk, ctx_v, bias):
    m = na_qk.shape[0]
    naw = NA_HEADS * NA_HEAD_DIM
    rows_n = m // GRID_W
    plan = _na_plan(rows_n)
    n_ctx = ctx_qk.shape[0]
    q_tok = NA_ROWS * GRID_W
    k_tok = plan.key_rows * GRID_W

    def slab(j):
        return jnp.clip(j * NA_ROWS - NA_WIN_H // 2, 0, rows_n - plan.key_rows) * GRID_W

    def variant(j):
        hi = j - (plan.n_blocks - plan.n_hi)
        return jnp.where(j < plan.n_lo, j, jnp.where(hi >= 0, plan.n_lo + 1 + hi, plan.n_lo))

    E = pl.Element
    return pl.pallas_call(
        _na_kernel,
        grid=(plan.n_blocks,),
        in_specs=[
            pl.BlockSpec((q_tok, naw), lambda j: (j, 0)),
            pl.BlockSpec((E(k_tok), E(naw)), lambda j: (slab(j), naw)),
            pl.BlockSpec((E(k_tok), E(naw)), lambda j: (slab(j), 0)),
            pl.BlockSpec((n_ctx, naw), lambda j: (0, 1)),
            pl.BlockSpec((n_ctx, naw), lambda j: (0, 0)),
            pl.BlockSpec((1, NA_HEADS, q_tok, k_tok + n_ctx), lambda j: (variant(j), 0, 0, 0)),
        ],
        out_specs=pl.BlockSpec((q_tok, naw), lambda j: (j, 0)),
        out_shape=jax.ShapeDtypeStruct((m, naw), BF16),
        compiler_params=_params("parallel"),
        name="na_attention",
    )(na_qk, na_qk, na_v, ctx_qk, ctx_v, bias)


def _merge_kernel(h_ref, yn_ref, ym_ref, wgn_ref, wgm_ref, wbn_ref, wbm_ref, o_ref):
    h = h_ref[...]
    gn = jax.nn.sigmoid(jnp.dot(h, wgn_ref[...], preferred_element_type=F32))
    gm = jax.nn.sigmoid(jnp.dot(h, wgm_ref[...], preferred_element_type=F32))
    a = jnp.dot(yn_ref[...], wbn_ref[...], preferred_element_type=F32)
    b = jnp.dot(ym_ref[...], wbm_ref[...], preferred_element_type=F32)
    o_ref[...] = (gn * a + gm * b).astype(o_ref.dtype)


def _merge(h, y_na, y_mla, lw):
    m, d = h.shape
    bm = _tile(m, 512)
    bn = _tile(d, 512, LANES)
    nj = d // bn
    kn, km = y_na.shape[1], y_mla.shape[1]
    return pl.pallas_call(
        _merge_kernel,
        grid=(nj, m // bm),
        in_specs=[pl.BlockSpec((bm, d), lambda j, i: (i, 0)),
                  pl.BlockSpec((bm, kn), lambda j, i: (i, 0)),
                  pl.BlockSpec((bm, km), lambda j, i: (i, 0)),
                  _w_spec(lw["w_gates"], bn, j_first=True),
                  _w_spec(lw["w_gates"], bn, shift=nj, j_first=True),
                  _w_spec(lw["w_b_na"], bn, j_first=True),
                  _w_spec(lw["w_b_mla"], bn, j_first=True)],
        out_specs=pl.BlockSpec((bm, bn), lambda j, i: (i, j)),
        out_shape=jax.ShapeDtypeStruct((m, d), BF16),
        compiler_params=_params("parallel", "parallel"),
        name="branch_merge",
    )(h, y_na, y_mla, lw["w_gates"].arr, lw["w_gates"].arr, lw["w_b_na"].arr, lw["w_b_mla"].arr)


def _residual_matmul(a, w, x, gate, name):
    m, d = x.shape
    bm = _tile(m, 1024)
    bn = _tile(d, 512, LANES)
    return _matmul(a, w, _epi_residual, [x, gate.reshape(1, d)],
                   [pl.BlockSpec((bm, bn), lambda i, j: (i, j)),
                    pl.BlockSpec((1, bn), lambda i, j: (0, j))],
                   jax.ShapeDtypeStruct((m, d), F32),
                   pl.BlockSpec((bm, bn), lambda i, j: (i, j)), bm, bn, name)


def _ffn_up_kernel(x_ref, wg_ref, wv_ref, halo_ref, cw_ref, cb_ref, o_ref):
    x = x_ref[...]
    g = jnp.dot(x, wg_ref[...], preferred_element_type=F32)
    bm = g.shape[0]
    rows = lax.broadcasted_iota(jnp.int32, g.shape, 0)
    g_prev = jnp.where(rows == 0, halo_ref[0:1, :], pltpu.roll(g, 1, axis=0))
    g_next = jnp.where(rows == bm - 1, halo_ref[1:2, :], pltpu.roll(g, bm - 1, axis=0))
    y = g_prev * cw_ref[0:1, :] + g * cw_ref[1:2, :] + g_next * cw_ref[2:3, :] + cb_ref[...]
    v = jnp.dot(x, wv_ref[...], preferred_element_type=F32)
    o_ref[...] = (y * jax.nn.sigmoid(y) * v).astype(o_ref.dtype)


def _ffn_up(h2, lw):
    m, d = h2.shape
    dff = lw["conv_b"].shape[1]
    bm = _tile(m, 1024)
    bn = _tile(dff, 512, LANES)
    nt, nj = m // bm, dff // bn

    edge = h2.reshape(nt, bm, d)[:, (0, bm - 1), :].reshape(2 * nt, d)
    pad = (-edge.shape[0]) % 16
    edge = jnp.pad(edge, ((0, pad), (0, 0)))
    bne = _tile(dff, 1024, LANES)
    eg = _matmul(edge, lw["w_up"]._replace(n=dff), _epi_cast, [], [],
                 jax.ShapeDtypeStruct((edge.shape[0], dff), F32),
                 pl.BlockSpec((edge.shape[0], bne), lambda i, j: (i, j)),
                 edge.shape[0], bne, "ffn_edge_rows")
    eg = eg[:2 * nt].reshape(nt, 2, dff)
    zero = jnp.zeros((1, dff), F32)
    prev_rows = jnp.concatenate([zero, eg[:-1, 1]], axis=0)
    next_rows = jnp.concatenate([eg[1:, 0], zero], axis=0)
    halo = jnp.stack([prev_rows, next_rows], axis=1)

    return pl.pallas_call(
        _ffn_up_kernel,
        grid=(m // bm, nj),
        in_specs=[pl.BlockSpec((bm, d), lambda i, j: (i, 0)),
                  _w_spec(lw["w_up"], bn),
                  _w_spec(lw["w_up"], bn, shift=nj),
                  pl.BlockSpec((None, 2, bn), lambda i, j: (i, 0, j)),
                  pl.BlockSpec((CONV_W, bn), lambda i, j: (0, j)),
                  pl.BlockSpec((1, bn), lambda i, j: (0, j))],
        out_specs=pl.BlockSpec((bm, bn), lambda i, j: (i, j)),
        out_shape=jax.ShapeDtypeStruct((m, dff), BF16),
        compiler_params=_params("parallel", "parallel"),
        name="ffn_up_conv",
    )(h2, lw["w_up"].arr, lw["w_up"].arr, halo, lw["conv_w"], lw["conv_b"])


def _rope_partner():
    half = MLA_ROPE_DIM // 2
    quarter = half // 2
    idx = np.arange(MLA_ROPE_DIM)
    return np.where((idx % half) < quarter, idx + quarter, idx - quarter)


def _pad_lanes(a):
    return jnp.pad(a, [(0, 0)] * (a.ndim - 1) + [(0, LANES - a.shape[-1])])


def _stacked_weights(w_in, mla_w_q_up, mla_w_kv_up, w_branch_na, w_branch_mla, w_out, ffn_w_up, ffn_w_down):
    depth = w_in.shape[0]
    naw = NA_HEADS * NA_HEAD_DIM
    q_rank = mla_w_q_up.shape[1]
    kv_rank = mla_w_kv_up.shape[1]
    perm = _rope_partner()
    o_cq = 3 * naw
    o_ckv = o_cq + q_rank
    o_kr = o_ckv + kv_rank
    o_g = o_kr + MLA_ROPE_DIM
    w_in = w_in.astype(BF16)
    w_kr = w_in[:, :, o_kr:o_g]
    wq = mla_w_q_up.reshape(depth, q_rank, MLA_HEADS, MLA_QK_DIM)
    wq_rope = wq[..., MLA_NOPE_DIM:]
    return {
        "in_head": w_in,
        "ckv_kr": jnp.concatenate([w_in[:, :, o_ckv:o_kr], _pad_lanes(w_kr), _pad_lanes(w_kr[:, :, perm])], axis=2),
        "gates": w_in[:, :, o_g:],
        "q_up": jnp.concatenate([wq[..., :MLA_NOPE_DIM], _pad_lanes(wq_rope), _pad_lanes(wq_rope[..., perm])],
                                axis=3).reshape(depth, q_rank, MLA_HEADS * 3 * LANES).astype(BF16),
        "kv_up": mla_w_kv_up.astype(BF16),
        "b_na": w_branch_na.astype(BF16),
        "b_mla": w_branch_mla.astype(BF16),
        "out": w_out.astype(BF16),
        "up": ffn_w_up.astype(BF16),
        "down": ffn_w_down.astype(BF16),
    }


def _layer_weights(l, sw, na_q_norm, na_k_norm, mla_cq_norm, mla_ckv_norm, mla_q_norm, mla_k_norm,
                   ffn_conv_w, ffn_conv_b):
    naw = NA_HEADS * NA_HEAD_DIM
    q_rank = sw["q_up"].shape[1]
    kv_rank = sw["kv_up"].shape[1]
    perm = _rope_partner()
    whole = lambda a: _W(a, l, 0, a.shape[2])
    lw = {
        "w_na_qk": _W(sw["in_head"], l, 0, 2 * naw),
        "w_na_v": _W(sw["in_head"], l, 2 * naw, naw),
        "w_cq": _W(sw["in_head"], l, 3 * naw, q_rank),
        "w_ckv_kr": whole(sw["ckv_kr"]),
        "w_gates": whole(sw["gates"]),
        "w_q_up": whole(sw["q_up"]),
        "w_kv_up": whole(sw["kv_up"]),
        "w_b_na": whole(sw["b_na"]),
        "w_b_mla": whole(sw["b_mla"]),
        "w_out": whole(sw["out"]),
        "w_up": whole(sw["up"]),
        "w_down": whole(sw["down"]),
        "g_na_qk": jnp.concatenate([jnp.tile(na_q_norm[l] * (NA_HEAD_DIM ** -0.5 * LOG2E), NA_HEADS),
                                    jnp.tile(na_k_norm[l], NA_HEADS)]).reshape(1, 2 * naw),
        "g_cq": mla_cq_norm[l].reshape(1, q_rank),
        "g_ckv": mla_ckv_norm[l].reshape(1, kv_rank),
        "conv_w": ffn_conv_w[l],
        "conv_b": ffn_conv_b[l].reshape(1, -1),
    }
    for tag, g in (("q", mla_q_norm[l]), ("k", mla_k_norm[l])):
        g_rope = g[MLA_NOPE_DIM:]
        lw[f"g{tag}_nope"] = g[:MLA_NOPE_DIM].reshape(1, LANES)
        lw[f"g{tag}_rope"] = _pad_lanes(g_rope).reshape(1, LANES)
        lw[f"g{tag}_swap"] = _pad_lanes(g_rope[perm]).reshape(1, LANES)
    return lw


def _rope_tables(n_tok):
    t = jnp.arange(n_tok, dtype=jnp.int32)
    half = MLA_ROPE_DIM // 2
    inv_freq = ROPE_THETA ** (-jnp.arange(0, half, 2, dtype=F32) / half)
    cs, sn = [], []
    for pos in (t // GRID_W, t % GRID_W):
        ang = pos.astype(F32)[:, None] * inv_freq[None, :]
        cs += [jnp.cos(ang), jnp.cos(ang)]
        sn += [-jnp.sin(ang), jnp.sin(ang)]
    return _pad_lanes(jnp.concatenate(cs, axis=1)), _pad_lanes(jnp.concatenate(sn, axis=1))


def _identity_tables(n_tok):
    cos_t = _pad_lanes(jnp.ones((n_tok, MLA_ROPE_DIM), F32))
    return cos_t, jnp.zeros((n_tok, LANES), F32)


def _head_major(a, heads):
    rows = a.shape[0]
    return a.reshape(rows, heads, -1).transpose(1, 0, 2)


def _head_major_vt(v, heads):
    rows = v.shape[0]
    vt = v.reshape(rows, heads, MLA_V_DIM).transpose(1, 2, 0)
    tail = jnp.zeros((heads, BF16_SUBLANES, rows), v.dtype).at[:, 0, :].set(1.0)
    return jnp.concatenate([vt, tail], axis=1)


def kernel(x, c, ctx, c_ctx, ada_down, ada_up, ada_bias, norm_mix, norm_ffn, w_in, na_q_norm, na_k_norm, na_rpb, mla_cq_norm, mla_ckv_norm, mla_w_q_up, mla_w_kv_up, mla_q_norm, mla_k_norm, w_branch_na, w_branch_mla, w_out, ffn_w_up, ffn_conv_w, ffn_conv_b, ffn_w_down):
    batch, n_tok, d = x.shape
    assert batch == 1 and c.shape[0] == 1
    depth = w_in.shape[0]
    naw = NA_HEADS * NA_HEAD_DIM
    xs = x[0]
    cs = ctx[0]
    lat_tabs = _rope_tables(n_tok)
    ctx_tabs = _identity_tables(cs.shape[0])
    cond = jnp.zeros((8, d), F32).at[0].set(c[0]).at[1].set(c_ctx)
    sw = _stacked_weights(w_in, mla_w_q_up, mla_w_kv_up, w_branch_na, w_branch_mla, w_out, ffn_w_up, ffn_w_down)

    for l in range(depth):
        last = l == depth - 1
        lw = _layer_weights(l, sw, na_q_norm, na_k_norm, mla_cq_norm, mla_ckv_norm, mla_q_norm, mla_k_norm,
                            ffn_conv_w, ffn_conv_b)
        mods = _adaln(cond, ada_down, ada_up, ada_bias, l)
        sh1, sc1, g1, sh2, sc2, g2 = [mods[0, i * d:(i + 1) * d] for i in range(N_MOD)]
        csh1, csc1, cg1, csh2, csc2, cg2 = [mods[1, i * d:(i + 1) * d] for i in range(N_MOD)]

        h = _norm_mod(xs, norm_mix[l], sc1, sh1)
        hc = _norm_mod(cs, norm_mix[l], csc1, csh1)
        na_qk, na_v, mq, mk, mv = _project(h, lw, lat_tabs)
        c_qk, c_v, cmq, cmk, cmv = _project(hc, lw, ctx_tabs)

        y_na = _na_attention(na_qk, na_v, c_qk, c_v, _na_bias_table(na_rpb[l], n_tok // GRID_W, cs.shape[0]))
        y_mla = _flash(mq, mk, mv, (cmk, cmv), name="mla_attention")
        u = _merge(h, y_na, y_mla, lw)
        xs = _residual_matmul(u, lw["w_out"], xs, g1, "out_proj")
        h2 = _norm_mod(xs, norm_ffn[l], sc2, sh2)
        xs = _residual_matmul(_ffn_up(h2, lw), lw["w_down"], xs, g2, "ffn_down")

        if not last:
            yc_na = _flash(_head_major(c_qk[:, :naw], NA_HEADS), _head_major(c_qk[:, naw:], NA_HEADS),
                           _head_major_vt(c_v, NA_HEADS), None, name="ctx_na_attention")
            yc_mla = _flash(cmq, cmk, cmv, None, name="ctx_mla_attention")
            uc = _merge(hc, yc_na, yc_mla, lw)
            cs = _residual_matmul(uc, lw["w_out"], cs, cg1, "ctx_out_proj")
            hc2 = _norm_mod(cs, norm_ffn[l], csc2, csh2)
            cs = _residual_matmul(_ffn_up(hc2, lw), lw["w_down"], cs, cg2, "ctx_ffn_down")
    return xs[None]
```

```python
import functools
from typing import NamedTuple

import numpy as np
import jax
import jax.numpy as jnp
from jax import lax
from jax.experimental import pallas as pl
from jax.experimental.pallas import tpu as pltpu

F32 = jnp.float32
BF16 = jnp.bfloat16

GRID_W = 64
EPS = 1e-6
NEG_INF = -1e30
N_MOD = 6

NA_HEADS = 16
NA_HEAD_DIM = 128
NA_WIN_H = 8
NA_WIN_W = 16
NA_GROUP = 1
NA_ROWS = 2

MLA_HEADS = 16
MLA_NOPE_DIM = 128
MLA_ROPE_DIM = 64
MLA_QK_DIM = MLA_NOPE_DIM + MLA_ROPE_DIM
MLA_V_DIM = 128
ROPE_THETA = 10000.0
CONV_W = 3

LANES = 128
MLA_QK_PAD = 2 * LANES
BF16_SUBLANES = 16
VT_ROWS = MLA_V_DIM + BF16_SUBLANES
LOG2E = 1.4426950408889634
VMEM_LIMIT_BYTES = 56 * 1024 * 1024


def _tile(dim, pref, mult=8):
    if dim <= pref:
        return dim
    t = (pref // mult) * mult
    while t >= mult:
        if dim % t == 0:
            return t
        t -= mult
    return dim


def _params(*sem):
    return pltpu.CompilerParams(dimension_semantics=sem, vmem_limit_bytes=VMEM_LIMIT_BYTES)


def _ada_down_kernel(c_ref, w_ref, o_ref):
    c = c_ref[...]
    a = (c * jax.nn.sigmoid(c)).astype(BF16)
    o_ref[...] = jnp.dot(a, w_ref[...].astype(BF16), preferred_element_type=F32)


def _ada_up_kernel(t_ref, w_ref, b_ref, o_ref):
    o_ref[...] = jnp.dot(t_ref[...].astype(BF16), w_ref[...].astype(BF16),
                         preferred_element_type=F32) + b_ref[...]


def _adaln(cond, down, up, bias, layer):
    rows, d = cond.shape
    rank = down.shape[2]
    n = up.shape[2]
    bn1 = _tile(rank, 512, LANES)
    t = pl.pallas_call(
        _ada_down_kernel,
        grid=(rank // bn1,),
        in_specs=[pl.BlockSpec((rows, d), lambda j: (0, 0)),
                  pl.BlockSpec((None, d, bn1), lambda j: (layer, 0, j))],
        out_specs=pl.BlockSpec((rows, bn1), lambda j: (0, j)),
        out_shape=jax.ShapeDtypeStruct((rows, rank), F32),
        compiler_params=_params("parallel"),
        name="ada_down",
    )(cond, down)
    bn2 = _tile(n, 4096, LANES)
    return pl.pallas_call(
        _ada_up_kernel,
        grid=(n // bn2,),
        in_specs=[pl.BlockSpec((rows, rank), lambda j: (0, 0)),
                  pl.BlockSpec((None, rank, bn2), lambda j: (layer, 0, j)),
                  pl.BlockSpec((None, 1, bn2), lambda j: (layer, 0, j))],
        out_specs=pl.BlockSpec((rows, bn2), lambda j: (0, j)),
        out_shape=jax.ShapeDtypeStruct((rows, n), F32),
        compiler_params=_params("parallel"),
        name="ada_up",
    )(t, up, bias.reshape(bias.shape[0], 1, n))


def _norm_mod_kernel(x_ref, g_ref, sc_ref, sh_ref, o_ref):
    x = x_ref[...]
    y = x * lax.rsqrt(jnp.mean(x * x, axis=-1, keepdims=True) + EPS) * g_ref[...]
    o_ref[...] = (y * (1.0 + sc_ref[...]) + sh_ref[...]).astype(o_ref.dtype)


def _norm_mod(x, g, sc, sh):
    m, d = x.shape
    bm = _tile(m, 512)
    vec = pl.BlockSpec((1, d), lambda i: (0, 0))
    return pl.pallas_call(
        _norm_mod_kernel,
        grid=(m // bm,),
        in_specs=[pl.BlockSpec((bm, d), lambda i: (i, 0)), vec, vec, vec],
        out_specs=pl.BlockSpec((bm, d), lambda i: (i, 0)),
        out_shape=jax.ShapeDtypeStruct((m, d), BF16),
        compiler_params=_params("parallel"),
        name="norm_mod",
    )(x, g.reshape(1, d), sc.reshape(1, d), sh.reshape(1, d))


def _mm_kernel(x_ref, w_ref, *rest, epilogue):
    acc = jnp.dot(x_ref[...], w_ref[...], preferred_element_type=F32)
    epilogue(acc, *rest)


class _W(NamedTuple):
    arr: jax.Array
    layer: int
    col0: int
    n: int


def _w_spec(w, bn, shift=0, j_first=False):
    blk0, rem = divmod(w.col0, bn)
    assert rem == 0 and w.n % bn == 0
    if j_first:
        return pl.BlockSpec((None, w.arr.shape[1], bn), lambda j, i: (w.layer, 0, blk0 + shift + j))
    return pl.BlockSpec((None, w.arr.shape[1], bn), lambda i, j: (w.layer, 0, blk0 + shift + j))


def _matmul(x, w, epilogue, extras, extra_specs, out_shapes, out_specs, bm, bn, name):
    m, k = x.shape
    return pl.pallas_call(
        functools.partial(_mm_kernel, epilogue=epilogue),
        grid=(m // bm, w.n // bn),
        in_specs=[pl.BlockSpec((bm, k), lambda i, j: (i, 0)), _w_spec(w, bn)] + list(extra_specs),
        out_specs=out_specs,
        out_shape=out_shapes,
        compiler_params=_params("parallel", "parallel"),
        name=name,
    )(x, w.arr, *extras)


def _epi_cast(acc, o_ref):
    o_ref[...] = acc.astype(o_ref.dtype)


def _epi_headnorm(acc, g_ref, o_ref):
    for c in range(acc.shape[1] // LANES):
        sl = slice(c * LANES, (c + 1) * LANES)
        a = acc[:, sl]
        r = lax.rsqrt(jnp.mean(a * a, axis=-1, keepdims=True) + EPS)
        o_ref[:, sl] = (a * r * g_ref[:, sl]).astype(o_ref.dtype)


def _epi_rownorm(acc, g_ref, o_ref):
    r = lax.rsqrt(jnp.mean(acc * acc, axis=-1, keepdims=True) + EPS)
    o_ref[...] = (acc * r * g_ref[...]).astype(o_ref.dtype)


def _epi_ckv(acc, g_ref, ckv_ref, kr_ref, *, kv_rank):
    a = acc[:, :kv_rank]
    r = lax.rsqrt(jnp.mean(a * a, axis=-1, keepdims=True) + EPS)
    ckv_ref[...] = (a * r * g_ref[...]).astype(ckv_ref.dtype)
    kr_ref[...] = acc[:, kv_rank:]


def _epi_q_up(acc, c_ref, s_ref, gn_ref, gr_ref, gs_ref, o_ref, *, heads, scale):
    cg = c_ref[...] * gr_ref[...]
    sg = s_ref[...] * gs_ref[...]
    for h in range(heads):
        base = h * 3 * LANES
        nope = acc[:, base:base + LANES]
        rp = acc[:, base + LANES:base + 2 * LANES]
        sw = acc[:, base + 2 * LANES:base + 3 * LANES]
        ss = jnp.sum(nope * nope, axis=-1, keepdims=True) + jnp.sum(rp * rp, axis=-1, keepdims=True)
        r = lax.rsqrt(ss * (1.0 / MLA_QK_DIM) + EPS) * scale
        o_ref[h, :, 0:LANES] = (nope * r * gn_ref[...]).astype(o_ref.dtype)
        o_ref[h, :, LANES:2 * LANES] = ((rp * cg + sw * sg) * r).astype(o_ref.dtype)


def _ones_row_tile(cols, dtype):
    rows = lax.broadcasted_iota(jnp.int32, (BF16_SUBLANES, cols), 0)
    return jnp.where(rows == 0, 1.0, 0.0).astype(dtype)


def _epi_kv_up(acc, kr_ref, c_ref, s_ref, gn_ref, gr_ref, gs_ref, k_ref, vt_ref, *, heads):
    kr = kr_ref[:, 0:LANES]
    krs = kr_ref[:, LANES:2 * LANES]
    ssr = jnp.sum(kr * kr, axis=-1, keepdims=True)
    rope = kr * (c_ref[...] * gr_ref[...]) + krs * (s_ref[...] * gs_ref[...])
    for h in range(heads):
        base = h * 2 * LANES
        nope = acc[:, base:base + LANES]
        ss = jnp.sum(nope * nope, axis=-1, keepdims=True) + ssr
        r = lax.rsqrt(ss * (1.0 / MLA_QK_DIM) + EPS)
        k_ref[h, :, 0:LANES] = (nope * r * gn_ref[...]).astype(k_ref.dtype)
        k_ref[h, :, LANES:2 * LANES] = (rope * r).astype(k_ref.dtype)
        vt_ref[h, 0:MLA_V_DIM, :] = acc[:, base + LANES:base + 2 * LANES].T.astype(vt_ref.dtype)
        vt_ref[h, MLA_V_DIM:VT_ROWS, :] = _ones_row_tile(acc.shape[0], vt_ref.dtype)


def _epi_residual(acc, x_ref, g_ref, o_ref):
    o_ref[...] = x_ref[...] + g_ref[...] * acc


def _project(h, lw, tabs):
    m, d = h.shape
    bm = _tile(m, 1024)
    naw = NA_HEADS * NA_HEAD_DIM
    row = lambda n: pl.BlockSpec((1, n), lambda i, j: (0, 0))

    bn = _tile(2 * naw, 1024, LANES)
    na_qk = _matmul(h, lw["w_na_qk"], _epi_headnorm, [lw["g_na_qk"]],
                    [pl.BlockSpec((1, bn), lambda i, j: (0, j))],
                    jax.ShapeDtypeStruct((m, 2 * naw), BF16),
                    pl.BlockSpec((bm, bn), lambda i, j: (i, j)), bm, bn, "na_qk_proj")
    bn = _tile(naw, 1024, LANES)
    na_v = _matmul(h, lw["w_na_v"], _epi_cast, [], [],
                   jax.ShapeDtypeStruct((m, naw), BF16),
                   pl.BlockSpec((bm, bn), lambda i, j: (i, j)), bm, bn, "na_v_proj")

    q_rank = lw["w_cq"].n
    kv_rank = lw["w_ckv_kr"].n - 2 * LANES
    bmc = _tile(m, 512)
    cqn = _matmul(h, lw["w_cq"], _epi_rownorm, [lw["g_cq"]], [row(q_rank)],
                  jax.ShapeDtypeStruct((m, q_rank), BF16),
                  pl.BlockSpec((bmc, q_rank), lambda i, j: (i, 0)), bmc, q_rank, "cq_proj")
    ckvn, kr = _matmul(h, lw["w_ckv_kr"], functools.partial(_epi_ckv, kv_rank=kv_rank),
                       [lw["g_ckv"]], [row(kv_rank)],
                       (jax.ShapeDtypeStruct((m, kv_rank), BF16),
                        jax.ShapeDtypeStruct((m, 2 * LANES), F32)),
                       (pl.BlockSpec((bmc, kv_rank), lambda i, j: (i, 0)),
                        pl.BlockSpec((bmc, 2 * LANES), lambda i, j: (i, 0))),
                       bmc, kv_rank + 2 * LANES, "ckv_proj")

    cos_t, sin_t = tabs
    hb = 4 if MLA_HEADS % 4 == 0 else 1
    bmq = _tile(m, 1024)
    tab = pl.BlockSpec((bmq, LANES), lambda i, j: (i, 0))
    vec = pl.BlockSpec((1, LANES), lambda i, j: (0, 0))
    mla_q = _matmul(cqn, lw["w_q_up"],
                    functools.partial(_epi_q_up, heads=hb, scale=MLA_QK_DIM ** -0.5 * LOG2E),
                    [cos_t, sin_t, lw["gq_nope"], lw["gq_rope"], lw["gq_swap"]],
                    [tab, tab, vec, vec, vec],
                    jax.ShapeDtypeStruct((MLA_HEADS, m, MLA_QK_PAD), BF16),
                    pl.BlockSpec((hb, bmq, MLA_QK_PAD), lambda i, j: (j, i, 0)),
                    bmq, hb * 3 * LANES, "mla_q_up")
    bmk = _tile(m, 1024)
    tab = pl.BlockSpec((bmk, LANES), lambda i, j: (i, 0))
    mla_k, mla_v = _matmul(ckvn, lw["w_kv_up"], functools.partial(_epi_kv_up, heads=hb),
                           [kr, cos_t, sin_t, lw["gk_nope"], lw["gk_rope"], lw["gk_swap"]],
                           [pl.BlockSpec((bmk, 2 * LANES), lambda i, j: (i, 0)), tab, tab, vec, vec, vec],
                           (jax.ShapeDtypeStruct((MLA_HEADS, m, MLA_QK_PAD), BF16),
                            jax.ShapeDtypeStruct((MLA_HEADS, VT_ROWS, m), BF16)),
                           (pl.BlockSpec((hb, bmk, MLA_QK_PAD), lambda i, j: (j, i, 0)),
                            pl.BlockSpec((hb, VT_ROWS, bmk), lambda i, j: (j, 0, i))),
                           bmk, hb * 2 * LANES, "mla_kv_up")
    return na_qk, na_v, mla_q, mla_k, mla_v


def _dot_nt(a, b):
    return lax.dot_general(a, b, (((1,), (1,)), ((), ())), preferred_element_type=F32)


def _softmax_pv(s_t, vt_blk, state):
    blk_max = jnp.max(s_t, axis=0, keepdims=True)
    m_new = blk_max if state is None else jnp.maximum(state[0], blk_max)
    p_t = jnp.exp2(s_t - m_new).astype(BF16)
    upd = jnp.dot(vt_blk, p_t, preferred_element_type=F32)
    if state is None:
        return m_new, upd
    return m_new, jnp.exp2(state[0] - m_new) * state[1] + upd


def _flash_kernel(*refs, tk, n_chunks, has_ctx):
    if has_ctx:
        q_ref, k_ref, vt_ref, kc_ref, vct_ref, o_ref, s_scr = refs
    else:
        q_ref, k_ref, vt_ref, o_ref, s_scr = refs
    q = q_ref[...]

    def chunk(c):
        return pl.ds(pl.multiple_of(c * tk, tk), tk)

    def scores(c, slot):
        s_scr[slot] = _dot_nt(k_ref[chunk(c), :], q)

    def consume(c, slot, state):
        return _softmax_pv(s_scr[slot], vt_ref[:, chunk(c)], state)

    first = 0 if has_ctx else 1
    n_pairs = (n_chunks - first) // 2
    if has_ctx:
        init_k, init_vt = kc_ref[...], vct_ref[...]
    else:
        init_k, init_vt = k_ref[0:tk, :], vt_ref[:, 0:tk]
    init_scores = _dot_nt(init_k, q)
    if n_pairs > 0:
        scores(first, 0)
    state = _softmax_pv(init_scores, init_vt, None)

    if n_pairs > 0:

        def pair(j, state, prefetch):
            c0 = first + 2 * j
            scores(c0 + 1, 1)
            state = consume(c0, 0, state)
            if prefetch:
                scores(c0 + 2, 0)
            return consume(c0 + 1, 1, state)

        state = lax.fori_loop(0, n_pairs - 1, lambda j, st: pair(j, st, True), state)
        state = pair(n_pairs - 1, state, False)
    if (n_chunks - first) % 2:
        last = slice((n_chunks - 1) * tk, n_chunks * tk)
        state = _softmax_pv(_dot_nt(k_ref[last, :], q), vt_ref[:, last], state)

    acc = state[1]
    o_t = acc[0:MLA_V_DIM, :] / acc[MLA_V_DIM:MLA_V_DIM + 1, :]
    o_ref[...] = o_t.T.astype(o_ref.dtype)


def _flash(q, k, vt, ctx_kv, *, name):
    heads, mq, dq = q.shape
    mk = k.shape[1]
    tq = _tile(mq, 512)
    tk = _tile(mk, 2048, LANES)
    q_spec = pl.BlockSpec((None, tq, dq), lambda h, i: (h, i, 0))
    whole = lambda a: pl.BlockSpec((None,) + a.shape[1:], lambda h, i: (h, 0, 0))
    in_specs = [q_spec, whole(k), whole(vt)]
    args = [q, k, vt]
    if ctx_kv is not None:
        in_specs += [whole(ctx_kv[0]), whole(ctx_kv[1])]
        args += list(ctx_kv)
    return pl.pallas_call(
        functools.partial(_flash_kernel, tk=tk, n_chunks=mk // tk, has_ctx=ctx_kv is not None),
        grid=(heads, mq // tq),
        in_specs=in_specs,
        out_specs=pl.BlockSpec((tq, MLA_V_DIM), lambda h, i: (i, h)),
        out_shape=jax.ShapeDtypeStruct((mq, heads * MLA_V_DIM), BF16),
        scratch_shapes=[pltpu.VMEM((2, tk, tq), F32)],
        compiler_params=_params("parallel", "parallel"),
        name=name,
    )(*args)


def _na_kernel(q_ref, k_ref, v_ref, kc_ref, vc_ref, b_ref, o_ref):
    head = lambda h: slice(h * NA_HEAD_DIM, (h + 1) * NA_HEAD_DIM)

    def group_scores(g):
        return [_dot_nt(q_ref[:, head(h)], jnp.concatenate([kc_ref[:, head(h)], k_ref[:, head(h)]], axis=0))
                + b_ref[0, h] for h in range(g * NA_GROUP, (g + 1) * NA_GROUP)]

    def group_softmax(scores):
        probs, inv_l = [], []
        for s in scores:
            p = jnp.exp2(s - jnp.max(s, axis=-1, keepdims=True))
            inv_l.append(1.0 / jnp.sum(p, axis=-1, keepdims=True))
            probs.append(p.astype(BF16))
        return probs, inv_l

    def group_values(g, probs, inv_l):
        out = []
        for i, h in enumerate(range(g * NA_GROUP, (g + 1) * NA_GROUP)):
            v_all = jnp.concatenate([vc_ref[:, head(h)], v_ref[:, head(h)]], axis=0)
            out.append(jnp.dot(probs[i], v_all, preferred_element_type=F32) * inv_l[i])
        return out

    n_groups = NA_HEADS // NA_GROUP
    sc = {g: group_scores(g) for g in range(min(2, n_groups))}
    sm = {0: group_softmax(sc.pop(0))}
    outs = []
    for g in range(n_groups):
        if g + 2 < n_groups:
            sc[g + 2] = group_scores(g + 2)
        if g + 1 < n_groups:
            sm[g + 1] = group_softmax(sc.pop(g + 1))
        outs += group_values(g, *sm.pop(g))
    o_ref[...] = jnp.concatenate(outs, axis=-1).astype(o_ref.dtype)


class _NaPlan(NamedTuple):
    n_blocks: int
    key_rows: int
    n_lo: int
    n_hi: int
    variants: tuple


def _na_window_start(r, rows_n):
    return min(max(r - NA_WIN_H // 2, 0), rows_n - NA_WIN_H)


def _na_slab_start(r0, rows_n, key_rows):
    return min(max(r0 - NA_WIN_H // 2, 0), rows_n - key_rows)


def _na_plan(rows_n):
    key_rows = NA_ROWS + NA_WIN_H - 1
    assert rows_n % NA_ROWS == 0 and rows_n >= key_rows
    n_blocks = rows_n // NA_ROWS

    def shape_of(j):
        r0 = j * NA_ROWS
        base = _na_slab_start(r0, rows_n, key_rows)
        return (base - r0,) + tuple(_na_window_start(r0 + a, rows_n) - base for a in range(NA_ROWS))

    shapes = [shape_of(j) for j in range(n_blocks)]
    interior = (-(NA_WIN_H // 2),) + tuple(range(NA_ROWS))
    inner = [j for j, s in enumerate(shapes) if s == interior]
    assert inner and inner == list(range(inner[0], inner[-1] + 1))
    n_lo, n_hi = inner[0], n_blocks - 1 - inner[-1]
    blocks = list(range(n_lo)) + [inner[0]] + list(range(n_blocks - n_hi, n_blocks))
    variants = tuple((j * NA_ROWS, _na_slab_start(j * NA_ROWS, rows_n, key_rows)) for j in blocks)
    return _NaPlan(n_blocks, key_rows, n_lo, n_hi, variants)


def _na_bias_table(rpb, rows_n, n_ctx):
    plan = _na_plan(rows_n)
    qc = np.arange(GRID_W)
    kc = np.arange(GRID_W)
    cs = np.clip(qc - NA_WIN_W // 2, 0, GRID_W - NA_WIN_W)
    col_ok = (kc[None, :] >= cs[:, None]) & (kc[None, :] < cs[:, None] + NA_WIN_W)
    col_idx = np.clip(kc[None, :] - qc[:, None] + NA_WIN_W - 1, 0, 2 * NA_WIN_W - 2)
    cols = rpb.astype(F32)[:, :, col_idx]
    cols = jnp.where(jnp.asarray(col_ok)[None, None], cols * LOG2E, NEG_INF)
    kr = plan.key_rows
    cols = jnp.pad(cols, ((0, 0), (kr, kr), (0, 0), (0, 0)), constant_values=NEG_INF)
    tables = []
    for r0, base in plan.variants:
        parts = []
        for a in range(NA_ROWS):
            q_row = r0 + a
            k_row = base + np.arange(kr)
            st = _na_window_start(q_row, rows_n)
            row_ok = (k_row >= st) & (k_row < st + NA_WIN_H)
            d0 = base - q_row + NA_WIN_H - 1 + kr
            run = jnp.where(jnp.asarray(row_ok)[None, :, None, None], cols[:, d0:d0 + kr], NEG_INF)
            parts.append(run.transpose(0, 2, 1, 3).reshape(NA_HEADS, GRID_W, kr * GRID_W))
        tables.append(jnp.concatenate(parts, axis=1))
    table = jnp.stack(tables)
    return jnp.pad(table, ((0, 0), (0, 0), (0, 0), (n_ctx, 0)))


def _na_attention(na_qk, na_v, ctx_qk, ctx_v, bias):
    m = na_qk.shape[0]
    naw = NA_HEADS * NA_HEAD_DIM
    rows_n = m // GRID_W
    plan = _na_plan(rows_n)
    n_ctx = ctx_qk.shape[0]
    q_tok = NA_ROWS * GRID_W
    k_tok = plan.key_rows * GRID_W

    def slab(j):
        return jnp.clip(j * NA_ROWS - NA_WIN_H // 2, 0, rows_n - plan.key_rows) * GRID_W

    def variant(j):
        hi = j - (plan.n_blocks - plan.n_hi)
        return jnp.where(j < plan.n_lo, j, jnp.where(hi >= 0, plan.n_lo + 1 + hi, plan.n_lo))

    E = pl.Element
    return pl.pallas_call(
        _na_kernel,
        grid=(plan.n_blocks,),
        in_specs=[
            pl.BlockSpec((q_tok, naw), lambda j: (j, 0)),
            pl.BlockSpec((E(k_tok), E(naw)), lambda j: (slab(j), naw)),
            pl.BlockSpec((E(k_tok), E(naw)), lambda j: (slab(j), 0)),
            pl.BlockSpec((n_ctx, naw), lambda j: (0, 1)),
            pl.BlockSpec((n_ctx, naw), lambda j: (0, 0)),
            pl.BlockSpec((1, NA_HEADS, q_tok, k_tok + n_ctx), lambda j: (variant(j), 0, 0, 0)),
        ],
        out_specs=pl.BlockSpec((q_tok, naw), lambda j: (j, 0)),
        out_shape=jax.ShapeDtypeStruct((m, naw), BF16),
        compiler_params=_params("parallel"),
        name="na_attention",
    )(na_qk, na_qk, na_v, ctx_qk, ctx_v, bias)


def _merge_kernel(h_ref, yn_ref, ym_ref, wgn_ref, wgm_ref, wbn_ref, wbm_ref, o_ref):
    h = h_ref[...]
    gn = jax.nn.sigmoid(jnp.dot(h, wgn_ref[...], preferred_element_type=F32))
    gm = jax.nn.sigmoid(jnp.dot(h, wgm_ref[...], preferred_element_type=F32))
    a = jnp.dot(yn_ref[...], wbn_ref[...], preferred_element_type=F32)
    b = jnp.dot(ym_ref[...], wbm_ref[...], preferred_element_type=F32)
    o_ref[...] = (gn * a + gm * b).astype(o_ref.dtype)


def _merge(h, y_na, y_mla, lw):
    m, d = h.shape
    bm = _tile(m, 512)
    bn = _tile(d, 512, LANES)
    nj = d // bn
    kn, km = y_na.shape[1], y_mla.shape[1]
    return pl.pallas_call(
        _merge_kernel,
        grid=(nj, m // bm),
        in_specs=[pl.BlockSpec((bm, d), lambda j, i: (i, 0)),
                  pl.BlockSpec((bm, kn), lambda j, i: (i, 0)),
                  pl.BlockSpec((bm, km), lambda j, i: (i, 0)),
                  _w_spec(lw["w_gates"], bn, j_first=True),
                  _w_spec(lw["w_gates"], bn, shift=nj, j_first=True),
                  _w_spec(lw["w_b_na"], bn, j_first=True),
                  _w_spec(lw["w_b_mla"], bn, j_first=True)],
        out_specs=pl.BlockSpec((bm, bn), lambda j, i: (i, j)),
        out_shape=jax.ShapeDtypeStruct((m, d), BF16),
        compiler_params=_params("parallel", "parallel"),
        name="branch_merge",
    )(h, y_na, y_mla, lw["w_gates"].arr, lw["w_gates"].arr, lw["w_b_na"].arr, lw["w_b_mla"].arr)


def _residual_matmul(a, w, x, gate, name):
    m, d = x.shape
    bm = _tile(m, 1024)
    bn = _tile(d, 512, LANES)
    return _matmul(a, w, _epi_residual, [x, gate.reshape(1, d)],
                   [pl.BlockSpec((bm, bn), lambda i, j: (i, j)),
                    pl.BlockSpec((1, bn), lambda i, j: (0, j))],
                   jax.ShapeDtypeStruct((m, d), F32),
                   pl.BlockSpec((bm, bn), lambda i, j: (i, j)), bm, bn, name)


def _ffn_up_kernel(x_ref, wg_ref, wv_ref, halo_ref, cw_ref, cb_ref, o_ref):
    x = x_ref[...]
    g = jnp.dot(x, wg_ref[...], preferred_element_type=F32)
    bm = g.shape[0]
    rows = lax.broadcasted_iota(jnp.int32, g.shape, 0)
    g_prev = jnp.where(rows == 0, halo_ref[0:1, :], pltpu.roll(g, 1, axis=0))
    g_next = jnp.where(rows == bm - 1, halo_ref[1:2, :], pltpu.roll(g, bm - 1, axis=0))
    y = g_prev * cw_ref[0:1, :] + g * cw_ref[1:2, :] + g_next * cw_ref[2:3, :] + cb_ref[...]
    v = jnp.dot(x, wv_ref[...], preferred_element_type=F32)
    o_ref[...] = (y * jax.nn.sigmoid(y) * v).astype(o_ref.dtype)


def _ffn_up(h2, lw):
    m, d = h2.shape
    dff = lw["conv_b"].shape[1]
    bm = _tile(m, 1024)
    bn = _tile(dff, 512, LANES)
    nt, nj = m // bm, dff // bn

    edge = h2.reshape(nt, bm, d)[:, (0, bm - 1), :].reshape(2 * nt, d)
    pad = (-edge.shape[0]) % 16
    edge = jnp.pad(edge, ((0, pad), (0, 0)))
    bne = _tile(dff, 1024, LANES)
    eg = _matmul(edge, lw["w_up"]._replace(n=dff), _epi_cast, [], [],
                 jax.ShapeDtypeStruct((edge.shape[0], dff), F32),
                 pl.BlockSpec((edge.shape[0], bne), lambda i, j: (i, j)),
                 edge.shape[0], bne, "ffn_edge_rows")
    eg = eg[:2 * nt].reshape(nt, 2, dff)
    zero = jnp.zeros((1, dff), F32)
    prev_rows = jnp.concatenate([zero, eg[:-1, 1]], axis=0)
    next_rows = jnp.concatenate([eg[1:, 0], zero], axis=0)
    halo = jnp.stack([prev_rows, next_rows], axis=1)

    return pl.pallas_call(
        _ffn_up_kernel,
        grid=(m // bm, nj),
        in_specs=[pl.BlockSpec((bm, d), lambda i, j: (i, 0)),
                  _w_spec(lw["w_up"], bn),
                  _w_spec(lw["w_up"], bn, shift=nj),
                  pl.BlockSpec((None, 2, bn), lambda i, j: (i, 0, j)),
                  pl.BlockSpec((CONV_W, bn), lambda i, j: (0, j)),
                  pl.BlockSpec((1, bn), lambda i, j: (0, j))],
        out_specs=pl.BlockSpec((bm, bn), lambda i, j: (i, j)),
        out_shape=jax.ShapeDtypeStruct((m, dff), BF16),
        compiler_params=_params("parallel", "parallel"),
        name="ffn_up_conv",
    )(h2, lw["w_up"].arr, lw["w_up"].arr, halo, lw["conv_w"], lw["conv_b"])


def _rope_partner():
    half = MLA_ROPE_DIM // 2
    quarter = half // 2
    idx = np.arange(MLA_ROPE_DIM)
    return np.where((idx % half) < quarter, idx + quarter, idx - quarter)


def _pad_lanes(a):
    return jnp.pad(a, [(0, 0)] * (a.ndim - 1) + [(0, LANES - a.shape[-1])])


def _stacked_weights(w_in, mla_w_q_up, mla_w_kv_up, w_branch_na, w_branch_mla, w_out, ffn_w_up, ffn_w_down):
    depth = w_in.shape[0]
    naw = NA_HEADS * NA_HEAD_DIM
    q_rank = mla_w_q_up.shape[1]
    kv_rank = mla_w_kv_up.shape[1]
    perm = _rope_partner()
    o_cq = 3 * naw
    o_ckv = o_cq + q_rank
    o_kr = o_ckv + kv_rank
    o_g = o_kr + MLA_ROPE_DIM
    w_in = w_in.astype(BF16)
    w_kr = w_in[:, :, o_kr:o_g]
    wq = mla_w_q_up.reshape(depth, q_rank, MLA_HEADS, MLA_QK_DIM)
    wq_rope = wq[..., MLA_NOPE_DIM:]
    return {
        "in_head": w_in,
        "ckv_kr": jnp.concatenate([w_in[:, :, o_ckv:o_kr], _pad_lanes(w_kr), _pad_lanes(w_kr[:, :, perm])], axis=2),
        "gates": w_in[:, :, o_g:],
        "q_up": jnp.concatenate([wq[..., :MLA_NOPE_DIM], _pad_lanes(wq_rope), _pad_lanes(wq_rope[..., perm])],
                                axis=3).reshape(depth, q_rank, MLA_HEADS * 3 * LANES).astype(BF16),
        "kv_up": mla_w_kv_up.astype(BF16),
        "b_na": w_branch_na.astype(BF16),
        "b_mla": w_branch_mla.astype(BF16),
        "out": w_out.astype(BF16),
        "up": ffn_w_up.astype(BF16),
        "down": ffn_w_down.astype(BF16),
    }


def _layer_weights(l, sw, na_q_norm, na_k_norm, mla_cq_norm, mla_ckv_norm, mla_q_norm, mla_k_norm,
                   ffn_conv_w, ffn_conv_b):
    naw = NA_HEADS * NA_HEAD_DIM
    q_rank = sw["q_up"].shape[1]
    kv_rank = sw["kv_up"].shape[1]
    perm = _rope_partner()
    whole = lambda a: _W(a, l, 0, a.shape[2])
    lw = {
        "w_na_qk": _W(sw["in_head"], l, 0, 2 * naw),
        "w_na_v": _W(sw["in_head"], l, 2 * naw, naw),
        "w_cq": _W(sw["in_head"], l, 3 * naw, q_rank),
        "w_ckv_kr": whole(sw["ckv_kr"]),
        "w_gates": whole(sw["gates"]),
        "w_q_up": whole(sw["q_up"]),
        "w_kv_up": whole(sw["kv_up"]),
        "w_b_na": whole(sw["b_na"]),
        "w_b_mla": whole(sw["b_mla"]),
        "w_out": whole(sw["out"]),
        "w_up": whole(sw["up"]),
        "w_down": whole(sw["down"]),
        "g_na_qk": jnp.concatenate([jnp.tile(na_q_norm[l] * (NA_HEAD_DIM ** -0.5 * LOG2E), NA_HEADS),
                                    jnp.tile(na_k_norm[l], NA_HEADS)]).reshape(1, 2 * naw),
        "g_cq": mla_cq_norm[l].reshape(1, q_rank),
        "g_ckv": mla_ckv_norm[l].reshape(1, kv_rank),
        "conv_w": ffn_conv_w[l],
        "conv_b": ffn_conv_b[l].reshape(1, -1),
    }
    for tag, g in (("q", mla_q_norm[l]), ("k", mla_k_norm[l])):
        g_rope = g[MLA_NOPE_DIM:]
        lw[f"g{tag}_nope"] = g[:MLA_NOPE_DIM].reshape(1, LANES)
        lw[f"g{tag}_rope"] = _pad_lanes(g_rope).reshape(1, LANES)
        lw[f"g{tag}_swap"] = _pad_lanes(g_rope[perm]).reshape(1, LANES)
    return lw


def _rope_tables(n_tok):
    t = jnp.arange(n_tok, dtype=jnp.int32)
    half = MLA_ROPE_DIM // 2
    inv_freq = ROPE_THETA ** (-jnp.arange(0, half, 2, dtype=F32) / half)
    cs, sn = [], []
    for pos in (t // GRID_W, t % GRID_W):
        ang = pos.astype(F32)[:, None] * inv_freq[None, :]
        cs += [jnp.cos(ang), jnp.cos(ang)]
        sn += [-jnp.sin(ang), jnp.sin(ang)]
    return _pad_lanes(jnp.concatenate(cs, axis=1)), _pad_lanes(jnp.concatenate(sn, axis=1))


def _identity_tables(n_tok):
    cos_t = _pad_lanes(jnp.ones((n_tok, MLA_ROPE_DIM), F32))
    return cos_t, jnp.zeros((n_tok, LANES), F32)


def _head_major(a, heads):
    rows = a.shape[0]
    return a.reshape(rows, heads, -1).transpose(1, 0, 2)


def _head_major_vt(v, heads):
    rows = v.shape[0]
    vt = v.reshape(rows, heads, MLA_V_DIM).transpose(1, 2, 0)
    tail = jnp.zeros((heads, BF16_SUBLANES, rows), v.dtype).at[:, 0, :].set(1.0)
    return jnp.concatenate([vt, tail], axis=1)


def kernel(x, c, ctx, c_ctx, ada_down, ada_up, ada_bias, norm_mix, norm_ffn, w_in, na_q_norm, na_k_norm, na_rpb, mla_cq_norm, mla_ckv_norm, mla_w_q_up, mla_w_kv_up, mla_q_norm, mla_k_norm, w_branch_na, w_branch_mla, w_out, ffn_w_up, ffn_conv_w, ffn_conv_b, ffn_w_down):
    batch, n_tok, d = x.shape
    assert batch == 1 and c.shape[0] == 1
    assert n_tok % GRID_W == 0 and NA_HEADS % NA_GROUP == 0
    depth = w_in.shape[0]
    naw = NA_HEADS * NA_HEAD_DIM
    xs = x[0]
    cs = ctx[0]
    lat_tabs = _rope_tables(n_tok)
    ctx_tabs = _identity_tables(cs.shape[0])
    cond = jnp.zeros((8, d), F32).at[0].set(c[0]).at[1].set(c_ctx)
    sw = _stacked_weights(w_in, mla_w_q_up, mla_w_kv_up, w_branch_na, w_branch_mla, w_out, ffn_w_up, ffn_w_down)

    for l in range(depth):
        last = l == depth - 1
        lw = _layer_weights(l, sw, na_q_norm, na_k_norm, mla_cq_norm, mla_ckv_norm, mla_q_norm, mla_k_norm,
                            ffn_conv_w, ffn_conv_b)
        mods = _adaln(cond, ada_down, ada_up, ada_bias, l)
        sh1, sc1, g1, sh2, sc2, g2 = [mods[0, i * d:(i + 1) * d] for i in range(N_MOD)]
        csh1, csc1, cg1, csh2, csc2, cg2 = [mods[1, i * d:(i + 1) * d] for i in range(N_MOD)]

        h = _norm_mod(xs, norm_mix[l], sc1, sh1)
        hc = _norm_mod(cs, norm_mix[l], csc1, csh1)
        na_qk, na_v, mq, mk, mv = _project(h, lw, lat_tabs)
        c_qk, c_v, cmq, cmk, cmv = _project(hc, lw, ctx_tabs)

        y_na = _na_attention(na_qk, na_v, c_qk, c_v, _na_bias_table(na_rpb[l], n_tok // GRID_W, cs.shape[0]))
        y_mla = _flash(mq, mk, mv, (cmk, cmv), name="mla_attention")
        u = _merge(h, y_na, y_mla, lw)
        xs = _residual_matmul(u, lw["w_out"], xs, g1, "out_proj")
        h2 = _norm_mod(xs, norm_ffn[l], sc2, sh2)
        xs = _residual_matmul(_ffn_up(h2, lw), lw["w_down"], xs, g2, "ffn_down")

        if not last:
            yc_na = _flash(_head_major(c_qk[:, :naw], NA_HEADS), _head_major(c_qk[:, naw:], NA_HEADS),
                           _head_major_vt(c_v, NA_HEADS), None, name="ctx_na_attention")
            yc_mla = _flash(cmq, cmk, cmv, None, name="ctx_mla_attention")
            uc = _merge(hc, yc_na, yc_mla, lw)
            cs = _residual_matmul(uc, lw["w_out"], cs, cg1, "ctx_out_proj")
            hc2 = _norm_mod(cs, norm_ffn[l], csc2, csh2)
            cs = _residual_matmul(_ffn_up(hc2, lw), lw["w_down"], cs, cg2, "ctx_ffn_down")
    return xs[None]
```

```python
import functools
from typing import NamedTuple

import numpy as np
import jax
import jax.numpy as jnp
from jax import lax
from jax.experimental import pallas as pl
from jax.experimental.pallas import tpu as pltpu

F32 = jnp.float32
BF16 = jnp.bfloat16

GRID_W = 64
EPS = 1e-6
NEG_INF = -1e30
N_MOD = 6

NA_HEADS = 16
NA_HEAD_DIM = 128
NA_WIN_H = 8
NA_WIN_W = 16
NA_GROUP = 1
NA_ROWS = 2

MLA_HEADS = 16
MLA_NOPE_DIM = 128
MLA_ROPE_DIM = 64
MLA_QK_DIM = MLA_NOPE_DIM + MLA_ROPE_DIM
MLA_V_DIM = 128
ROPE_THETA = 10000.0
CONV_W = 3

LANES = 128
MLA_QK_PAD = 2 * LANES
BF16_SUBLANES = 16
VT_ROWS = MLA_V_DIM + BF16_SUBLANES
LOG2E = 1.4426950408889634
VMEM_LIMIT_BYTES = 56 * 1024 * 1024


def _tile(dim, pref, mult=8):
    if dim <= pref:
        return dim
    t = (pref // mult) * mult
    while t >= mult:
        if dim % t == 0:
            return t
        t -= mult
    return dim


def _params(*sem):
    return pltpu.CompilerParams(dimension_semantics=sem, vmem_limit_bytes=VMEM_LIMIT_BYTES)


def _ada_down_kernel(c_ref, w_ref, o_ref):
    c = c_ref[...]
    a = (c * jax.nn.sigmoid(c)).astype(BF16)
    o_ref[...] = jnp.dot(a, w_ref[...].astype(BF16), preferred_element_type=F32)


def _ada_up_kernel(t_ref, w_ref, b_ref, o_ref):
    o_ref[...] = jnp.dot(t_ref[...].astype(BF16), w_ref[...].astype(BF16),
                         preferred_element_type=F32) + b_ref[...]


def _adaln(cond, down, up, bias, layer):
    rows, d = cond.shape
    rank = down.shape[2]
    n = up.shape[2]
    bn1 = _tile(rank, 512, LANES)
    t = pl.pallas_call(
        _ada_down_kernel,
        grid=(rank // bn1,),
        in_specs=[pl.BlockSpec((rows, d), lambda j: (0, 0)),
                  pl.BlockSpec((None, d, bn1), lambda j: (layer, 0, j))],
        out_specs=pl.BlockSpec((rows, bn1), lambda j: (0, j)),
        out_shape=jax.ShapeDtypeStruct((rows, rank), F32),
        compiler_params=_params("parallel"),
        name="ada_down",
    )(cond, down)
    bn2 = _tile(n, 4096, LANES)
    return pl.pallas_call(
        _ada_up_kernel,
        grid=(n // bn2,),
        in_specs=[pl.BlockSpec((rows, rank), lambda j: (0, 0)),
                  pl.BlockSpec((None, rank, bn2), lambda j: (layer, 0, j)),
                  pl.BlockSpec((None, 1, bn2), lambda j: (layer, 0, j))],
        out_specs=pl.BlockSpec((rows, bn2), lambda j: (0, j)),
        out_shape=jax.ShapeDtypeStruct((rows, n), F32),
        compiler_params=_params("parallel"),
        name="ada_up",
    )(t, up, bias.reshape(bias.shape[0], 1, n))


def _norm_mod_kernel(x_ref, g_ref, sc_ref, sh_ref, o_ref):
    x = x_ref[...]
    y = x * lax.rsqrt(jnp.mean(x * x, axis=-1, keepdims=True) + EPS) * g_ref[...]
    o_ref[...] = (y * (1.0 + sc_ref[...]) + sh_ref[...]).astype(o_ref.dtype)


def _norm_mod(x, g, sc, sh):
    m, d = x.shape
    bm = _tile(m, 512)
    vec = pl.BlockSpec((1, d), lambda i: (0, 0))
    return pl.pallas_call(
        _norm_mod_kernel,
        grid=(m // bm,),
        in_specs=[pl.BlockSpec((bm, d), lambda i: (i, 0)), vec, vec, vec],
        out_specs=pl.BlockSpec((bm, d), lambda i: (i, 0)),
        out_shape=jax.ShapeDtypeStruct((m, d), BF16),
        compiler_params=_params("parallel"),
        name="norm_mod",
    )(x, g.reshape(1, d), sc.reshape(1, d), sh.reshape(1, d))


def _mm_kernel(x_ref, w_ref, *rest, epilogue):
    acc = jnp.dot(x_ref[...], w_ref[...], preferred_element_type=F32)
    epilogue(acc, *rest)


class _W(NamedTuple):
    arr: jax.Array
    layer: int
    col0: int
    n: int


def _w_spec(w, bn, shift=0, j_first=False):
    blk0, rem = divmod(w.col0, bn)
    assert rem == 0 and w.n % bn == 0
    if j_first:
        return pl.BlockSpec((None, w.arr.shape[1], bn), lambda j, i: (w.layer, 0, blk0 + shift + j))
    return pl.BlockSpec((None, w.arr.shape[1], bn), lambda i, j: (w.layer, 0, blk0 + shift + j))


def _matmul(x, w, epilogue, extras, extra_specs, out_shapes, out_specs, bm, bn, name):
    m, k = x.shape
    return pl.pallas_call(
        functools.partial(_mm_kernel, epilogue=epilogue),
        grid=(m // bm, w.n // bn),
        in_specs=[pl.BlockSpec((bm, k), lambda i, j: (i, 0)), _w_spec(w, bn)] + list(extra_specs),
        out_specs=out_specs,
        out_shape=out_shapes,
        compiler_params=_params("parallel", "parallel"),
        name=name,
    )(x, w.arr, *extras)


def _epi_cast(acc, o_ref):
    o_ref[...] = acc.astype(o_ref.dtype)


def _epi_headnorm(acc, g_ref, o_ref):
    for c in range(acc.shape[1] // LANES):
        sl = slice(c * LANES, (c + 1) * LANES)
        a = acc[:, sl]
        r = lax.rsqrt(jnp.mean(a * a, axis=-1, keepdims=True) + EPS)
        o_ref[:, sl] = (a * r * g_ref[:, sl]).astype(o_ref.dtype)


def _epi_rownorm(acc, g_ref, o_ref):
    r = lax.rsqrt(jnp.mean(acc * acc, axis=-1, keepdims=True) + EPS)
    o_ref[...] = (acc * r * g_ref[...]).astype(o_ref.dtype)


def _epi_ckv(acc, g_ref, ckv_ref, kr_ref, *, kv_rank):
    a = acc[:, :kv_rank]
    r = lax.rsqrt(jnp.mean(a * a, axis=-1, keepdims=True) + EPS)
    ckv_ref[...] = (a * r * g_ref[...]).astype(ckv_ref.dtype)
    kr_ref[...] = acc[:, kv_rank:]


def _epi_q_up(acc, c_ref, s_ref, gn_ref, gr_ref, gs_ref, o_ref, *, heads, scale):
    cg = c_ref[...] * gr_ref[...]
    sg = s_ref[...] * gs_ref[...]
    for h in range(heads):
        base = h * 3 * LANES
        nope = acc[:, base:base + LANES]
        rp = acc[:, base + LANES:base + 2 * LANES]
        sw = acc[:, base + 2 * LANES:base + 3 * LANES]
        ss = jnp.sum(nope * nope, axis=-1, keepdims=True) + jnp.sum(rp * rp, axis=-1, keepdims=True)
        r = lax.rsqrt(ss * (1.0 / MLA_QK_DIM) + EPS) * scale
        o_ref[h, :, 0:LANES] = (nope * r * gn_ref[...]).astype(o_ref.dtype)
        o_ref[h, :, LANES:2 * LANES] = ((rp * cg + sw * sg) * r).astype(o_ref.dtype)


def _ones_row_tile(cols, dtype):
    rows = lax.broadcasted_iota(jnp.int32, (BF16_SUBLANES, cols), 0)
    return jnp.where(rows == 0, 1.0, 0.0).astype(dtype)


def _epi_kv_up(acc, kr_ref, c_ref, s_ref, gn_ref, gr_ref, gs_ref, k_ref, vt_ref, *, heads):
    kr = kr_ref[:, 0:LANES]
    krs = kr_ref[:, LANES:2 * LANES]
    ssr = jnp.sum(kr * kr, axis=-1, keepdims=True)
    rope = kr * (c_ref[...] * gr_ref[...]) + krs * (s_ref[...] * gs_ref[...])
    for h in range(heads):
        base = h * 2 * LANES
        nope = acc[:, base:base + LANES]
        ss = jnp.sum(nope * nope, axis=-1, keepdims=True) + ssr
        r = lax.rsqrt(ss * (1.0 / MLA_QK_DIM) + EPS)
        k_ref[h, :, 0:LANES] = (nope * r * gn_ref[...]).astype(k_ref.dtype)
        k_ref[h, :, LANES:2 * LANES] = (rope * r).astype(k_ref.dtype)
        vt_ref[h, 0:MLA_V_DIM, :] = acc[:, base + LANES:base + 2 * LANES].T.astype(vt_ref.dtype)
        vt_ref[h, MLA_V_DIM:VT_ROWS, :] = _ones_row_tile(acc.shape[0], vt_ref.dtype)


def _epi_residual(acc, x_ref, g_ref, o_ref):
    o_ref[...] = x_ref[...] + g_ref[...] * acc


def _project(h, lw, tabs):
    m, d = h.shape
    bm = _tile(m, 1024)
    naw = NA_HEADS * NA_HEAD_DIM
    row = lambda n: pl.BlockSpec((1, n), lambda i, j: (0, 0))

    bn = _tile(2 * naw, 1024, LANES)
    na_qk = _matmul(h, lw["w_na_qk"], _epi_headnorm, [lw["g_na_qk"]],
                    [pl.BlockSpec((1, bn), lambda i, j: (0, j))],
                    jax.ShapeDtypeStruct((m, 2 * naw), BF16),
                    pl.BlockSpec((bm, bn), lambda i, j: (i, j)), bm, bn, "na_qk_proj")
    bn = _tile(naw, 1024, LANES)
    na_v = _matmul(h, lw["w_na_v"], _epi_cast, [], [],
                   jax.ShapeDtypeStruct((m, naw), BF16),
                   pl.BlockSpec((bm, bn), lambda i, j: (i, j)), bm, bn, "na_v_proj")

    q_rank = lw["w_cq"].n
    kv_rank = lw["w_ckv_kr"].n - 2 * LANES
    bmc = _tile(m, 512)
    cqn = _matmul(h, lw["w_cq"], _epi_rownorm, [lw["g_cq"]], [row(q_rank)],
                  jax.ShapeDtypeStruct((m, q_rank), BF16),
                  pl.BlockSpec((bmc, q_rank), lambda i, j: (i, 0)), bmc, q_rank, "cq_proj")
    ckvn, kr = _matmul(h, lw["w_ckv_kr"], functools.partial(_epi_ckv, kv_rank=kv_rank),
                       [lw["g_ckv"]], [row(kv_rank)],
                       (jax.ShapeDtypeStruct((m, kv_rank), BF16),
                        jax.ShapeDtypeStruct((m, 2 * LANES), F32)),
                       (pl.BlockSpec((bmc, kv_rank), lambda i, j: (i, 0)),
                        pl.BlockSpec((bmc, 2 * LANES), lambda i, j: (i, 0))),
                       bmc, kv_rank + 2 * LANES, "ckv_proj")

    cos_t, sin_t = tabs
    hb = 4 if MLA_HEADS % 4 == 0 else 1
    bmq = _tile(m, 1024)
    tab = pl.BlockSpec((bmq, LANES), lambda i, j: (i, 0))
    vec = pl.BlockSpec((1, LANES), lambda i, j: (0, 0))
    mla_q = _matmul(cqn, lw["w_q_up"],
                    functools.partial(_epi_q_up, heads=hb, scale=MLA_QK_DIM ** -0.5 * LOG2E),
                    [cos_t, sin_t, lw["gq_nope"], lw["gq_rope"], lw["gq_swap"]],
                    [tab, tab, vec, vec, vec],
                    jax.ShapeDtypeStruct((MLA_HEADS, m, MLA_QK_PAD), BF16),
                    pl.BlockSpec((hb, bmq, MLA_QK_PAD), lambda i, j: (j, i, 0)),
                    bmq, hb * 3 * LANES, "mla_q_up")
    bmk = _tile(m, 1024)
    tab = pl.BlockSpec((bmk, LANES), lambda i, j: (i, 0))
    mla_k, mla_v = _matmul(ckvn, lw["w_kv_up"], functools.partial(_epi_kv_up, heads=hb),
                           [kr, cos_t, sin_t, lw["gk_nope"], lw["gk_rope"], lw["gk_swap"]],
                           [pl.BlockSpec((bmk, 2 * LANES), lambda i, j: (i, 0)), tab, tab, vec, vec, vec],
                           (jax.ShapeDtypeStruct((MLA_HEADS, m, MLA_QK_PAD), BF16),
                            jax.ShapeDtypeStruct((MLA_HEADS, VT_ROWS, m), BF16)),
                           (pl.BlockSpec((hb, bmk, MLA_QK_PAD), lambda i, j: (j, i, 0)),
                            pl.BlockSpec((hb, VT_ROWS, bmk), lambda i, j: (j, 0, i))),
                           bmk, hb * 2 * LANES, "mla_kv_up")
    return na_qk, na_v, mla_q, mla_k, mla_v


def _dot_nt(a, b):
    return lax.dot_general(a, b, (((1,), (1,)), ((), ())), preferred_element_type=F32)


def _softmax_pv(s_t, vt_blk, state):
    blk_max = jnp.max(s_t, axis=0, keepdims=True)
    m_new = blk_max if state is None else jnp.maximum(state[0], blk_max)
    p_t = jnp.exp2(s_t - m_new).astype(BF16)
    upd = jnp.dot(vt_blk, p_t, preferred_element_type=F32)
    if state is None:
        return m_new, upd
    return m_new, jnp.exp2(state[0] - m_new) * state[1] + upd


def _flash_kernel(*refs, tk, n_chunks, has_ctx):
    if has_ctx:
        q_ref, k_ref, vt_ref, kc_ref, vct_ref, o_ref, s_scr = refs
    else:
        q_ref, k_ref, vt_ref, o_ref, s_scr = refs
    q = q_ref[...]

    def chunk(c):
        return pl.ds(pl.multiple_of(c * tk, tk), tk)

    def scores(c, slot):
        s_scr[slot] = _dot_nt(k_ref[chunk(c), :], q)

    def consume(c, slot, state):
        return _softmax_pv(s_scr[slot], vt_ref[:, chunk(c)], state)

    first = 0 if has_ctx else 1
    n_pairs = (n_chunks - first) // 2
    if has_ctx:
        init_k, init_vt = kc_ref[...], vct_ref[...]
    else:
        init_k, init_vt = k_ref[0:tk, :], vt_ref[:, 0:tk]
    init_scores = _dot_nt(init_k, q)
    if n_pairs > 0:
        scores(first, 0)
    state = _softmax_pv(init_scores, init_vt, None)

    if n_pairs > 0:

        def pair(j, state):
            c0 = first + 2 * j
            scores(c0 + 1, 1)
            state = consume(c0, 0, state)
            scores(jnp.minimum(c0 + 2, n_chunks - 1), 0)
            return consume(c0 + 1, 1, state)

        state = lax.fori_loop(0, n_pairs, pair, state)
    if (n_chunks - first) % 2:
        last = slice((n_chunks - 1) * tk, n_chunks * tk)
        state = _softmax_pv(_dot_nt(k_ref[last, :], q), vt_ref[:, last], state)

    acc = state[1]
    o_t = acc[0:MLA_V_DIM, :] / acc[MLA_V_DIM:MLA_V_DIM + 1, :]
    o_ref[...] = o_t.T.astype(o_ref.dtype)


def _flash(q, k, vt, ctx_kv, *, name):
    heads, mq, dq = q.shape
    mk = k.shape[1]
    tq = _tile(mq, 512)
    tk = _tile(mk, 2048, LANES)
    q_spec = pl.BlockSpec((None, tq, dq), lambda h, i: (h, i, 0))
    whole = lambda a: pl.BlockSpec((None,) + a.shape[1:], lambda h, i: (h, 0, 0))
    in_specs = [q_spec, whole(k), whole(vt)]
    args = [q, k, vt]
    if ctx_kv is not None:
        in_specs += [whole(ctx_kv[0]), whole(ctx_kv[1])]
        args += list(ctx_kv)
    return pl.pallas_call(
        functools.partial(_flash_kernel, tk=tk, n_chunks=mk // tk, has_ctx=ctx_kv is not None),
        grid=(heads, mq // tq),
        in_specs=in_specs,
        out_specs=pl.BlockSpec((tq, MLA_V_DIM), lambda h, i: (i, h)),
        out_shape=jax.ShapeDtypeStruct((mq, heads * MLA_V_DIM), BF16),
        scratch_shapes=[pltpu.VMEM((2, tk, tq), F32)],
        compiler_params=_params("parallel", "parallel"),
        name=name,
    )(*args)


def _na_kernel(q_ref, k_ref, v_ref, kc_ref, vc_ref, b_ref, o_ref):
    head = lambda h: slice(h * NA_HEAD_DIM, (h + 1) * NA_HEAD_DIM)

    def group_scores(g):
        return [_dot_nt(q_ref[:, head(h)], jnp.concatenate([kc_ref[:, head(h)], k_ref[:, head(h)]], axis=0))
                + b_ref[0, h] for h in range(g * NA_GROUP, (g + 1) * NA_GROUP)]

    def group_softmax(scores):
        probs, inv_l = [], []
        for s in scores:
            p = jnp.exp2(s - jnp.max(s, axis=-1, keepdims=True))
            inv_l.append(1.0 / jnp.sum(p, axis=-1, keepdims=True))
            probs.append(p.astype(BF16))
        return probs, inv_l

    def group_values(g, probs, inv_l):
        out = []
        for i, h in enumerate(range(g * NA_GROUP, (g + 1) * NA_GROUP)):
            v_all = jnp.concatenate([vc_ref[:, head(h)], v_ref[:, head(h)]], axis=0)
            out.append(jnp.dot(probs[i], v_all, preferred_element_type=F32) * inv_l[i])
        return out

    n_groups = NA_HEADS // NA_GROUP
    sc = {g: group_scores(g) for g in range(min(2, n_groups))}
    sm = {0: group_softmax(sc.pop(0))}
    outs = []
    for g in range(n_groups):
        if g + 2 < n_groups:
            sc[g + 2] = group_scores(g + 2)
        if g + 1 < n_groups:
            sm[g + 1] = group_softmax(sc.pop(g + 1))
        outs += group_values(g, *sm.pop(g))
    o_ref[...] = jnp.concatenate(outs, axis=-1).astype(o_ref.dtype)


class _NaPlan(NamedTuple):
    n_blocks: int
    key_rows: int
    n_lo: int
    n_hi: int
    variants: tuple


def _na_window_start(r, rows_n):
    return min(max(r - NA_WIN_H // 2, 0), rows_n - NA_WIN_H)


def _na_slab_start(r0, rows_n, key_rows):
    return min(max(r0 - NA_WIN_H // 2, 0), rows_n - key_rows)


def _na_plan(rows_n):
    key_rows = NA_ROWS + NA_WIN_H - 1
    assert rows_n % NA_ROWS == 0 and rows_n >= key_rows
    n_blocks = rows_n // NA_ROWS

    def shape_of(j):
        r0 = j * NA_ROWS
        base = _na_slab_start(r0, rows_n, key_rows)
        return (base - r0,) + tuple(_na_window_start(r0 + a, rows_n) - base for a in range(NA_ROWS))

    shapes = [shape_of(j) for j in range(n_blocks)]
    interior = (-(NA_WIN_H // 2),) + tuple(range(NA_ROWS))
    inner = [j for j, s in enumerate(shapes) if s == interior]
    assert inner and inner == list(range(inner[0], inner[-1] + 1))
    n_lo, n_hi = inner[0], n_blocks - 1 - inner[-1]
    blocks = list(range(n_lo)) + [inner[0]] + list(range(n_blocks - n_hi, n_blocks))
    variants = tuple((j * NA_ROWS, _na_slab_start(j * NA_ROWS, rows_n, key_rows)) for j in blocks)
    return _NaPlan(n_blocks, key_rows, n_lo, n_hi, variants)


def _na_bias_table(rpb, rows_n, n_ctx):
    plan = _na_plan(rows_n)
    qc = np.arange(GRID_W)
    kc = np.arange(GRID_W)
    cs = np.clip(qc - NA_WIN_W // 2, 0, GRID_W - NA_WIN_W)
    col_ok = (kc[None, :] >= cs[:, None]) & (kc[None, :] < cs[:, None] + NA_WIN_W)
    col_idx = np.clip(kc[None, :] - qc[:, None] + NA_WIN_W - 1, 0, 2 * NA_WIN_W - 2)
    cols = rpb.astype(F32)[:, :, col_idx]
    cols = jnp.where(jnp.asarray(col_ok)[None, None], cols * LOG2E, NEG_INF)
    kr = plan.key_rows
    cols = jnp.pad(cols, ((0, 0), (kr, kr), (0, 0), (0, 0)), constant_values=NEG_INF)
    tables = []
    for r0, base in plan.variants:
        parts = []
        for a in range(NA_ROWS):
            q_row = r0 + a
            k_row = base + np.arange(kr)
            st = _na_window_start(q_row, rows_n)
            row_ok = (k_row >= st) & (k_row < st + NA_WIN_H)
            d0 = base - q_row + NA_WIN_H - 1 + kr
            run = jnp.where(jnp.asarray(row_ok)[None, :, None, None], cols[:, d0:d0 + kr], NEG_INF)
            parts.append(run.transpose(0, 2, 1, 3).reshape(NA_HEADS, GRID_W, kr * GRID_W))
        tables.append(jnp.concatenate(parts, axis=1))
    table = jnp.stack(tables)
    return jnp.pad(table, ((0, 0), (0, 0), (0, 0), (n_ctx, 0)))


def _na_attention(na_qk, na_v, ctx_qk, ctx_v, bias):
    m = na_qk.shape[0]
    naw = NA_HEADS * NA_HEAD_DIM
    rows_n = m // GRID_W
    plan = _na_plan(rows_n)
    n_ctx = ctx_qk.shape[0]
    q_tok = NA_ROWS * GRID_W
    k_tok = plan.key_rows * GRID_W

    def slab(j):
        return jnp.clip(j * NA_ROWS - NA_WIN_H // 2, 0, rows_n - plan.key_rows) * GRID_W

    def variant(j):
        hi = j - (plan.n_blocks - plan.n_hi)
        return jnp.where(j < plan.n_lo, j, jnp.where(hi >= 0, plan.n_lo + 1 + hi, plan.n_lo))

    E = pl.Element
    return pl.pallas_call(
        _na_kernel,
        grid=(plan.n_blocks,),
        in_specs=[
            pl.BlockSpec((q_tok, naw), lambda j: (j, 0)),
            pl.BlockSpec((E(k_tok), E(naw)), lambda j: (slab(j), naw)),
            pl.BlockSpec((E(k_tok), E(naw)), lambda j: (slab(j), 0)),
            pl.BlockSpec((n_ctx, naw), lambda j: (0, 1)),
            pl.BlockSpec((n_ctx, naw), lambda j: (0, 0)),
            pl.BlockSpec((1, NA_HEADS, q_tok, k_tok + n_ctx), lambda j: (variant(j), 0, 0, 0)),
        ],
        out_specs=pl.BlockSpec((q_tok, naw), lambda j: (j, 0)),
        out_shape=jax.ShapeDtypeStruct((m, naw), BF16),
        compiler_params=_params("parallel"),
        name="na_attention",
    )(na_qk, na_qk, na_v, ctx_qk, ctx_v, bias)


def _merge_kernel(h_ref, yn_ref, ym_ref, wgn_ref, wgm_ref, wbn_ref, wbm_ref, o_ref):
    h = h_ref[...]
    gn = jax.nn.sigmoid(jnp.dot(h, wgn_ref[...], preferred_element_type=F32))
    gm = jax.nn.sigmoid(jnp.dot(h, wgm_ref[...], preferred_element_type=F32))
    a = jnp.dot(yn_ref[...], wbn_ref[...], preferred_element_type=F32)
    b = jnp.dot(ym_ref[...], wbm_ref[...], preferred_element_type=F32)
    o_ref[...] = (gn * a + gm * b).astype(o_ref.dtype)


def _merge(h, y_na, y_mla, lw):
    m, d = h.shape
    bm = _tile(m, 512)
    bn = _tile(d, 512, LANES)
    nj = d // bn
    kn, km = y_na.shape[1], y_mla.shape[1]
    return pl.pallas_call(
        _merge_kernel,
        grid=(nj, m // bm),
        in_specs=[pl.BlockSpec((bm, d), lambda j, i: (i, 0)),
                  pl.BlockSpec((bm, kn), lambda j, i: (i, 0)),
                  pl.BlockSpec((bm, km), lambda j, i: (i, 0)),
                  _w_spec(lw["w_gates"], bn, j_first=True),
                  _w_spec(lw["w_gates"], bn, shift=nj, j_first=True),
                  _w_spec(lw["w_b_na"], bn, j_first=True),
                  _w_spec(lw["w_b_mla"], bn, j_first=True)],
        out_specs=pl.BlockSpec((bm, bn), lambda j, i: (i, j)),
        out_shape=jax.ShapeDtypeStruct((m, d), BF16),
        compiler_params=_params("parallel", "parallel"),
        name="branch_merge",
    )(h, y_na, y_mla, lw["w_gates"].arr, lw["w_gates"].arr, lw["w_b_na"].arr, lw["w_b_mla"].arr)


def _residual_matmul(a, w, x, gate, name):
    m, d = x.shape
    bm = _tile(m, 1024)
    bn = _tile(d, 512, LANES)
    return _matmul(a, w, _epi_residual, [x, gate.reshape(1, d)],
                   [pl.BlockSpec((bm, bn), lambda i, j: (i, j)),
                    pl.BlockSpec((1, bn), lambda i, j: (0, j))],
                   jax.ShapeDtypeStruct((m, d), F32),
                   pl.BlockSpec((bm, bn), lambda i, j: (i, j)), bm, bn, name)


def _ffn_up_kernel(x_ref, wg_ref, wv_ref, halo_ref, cw_ref, cb_ref, o_ref):
    x = x_ref[...]
    g = jnp.dot(x, wg_ref[...], preferred_element_type=F32)
    bm = g.shape[0]
    rows = lax.broadcasted_iota(jnp.int32, g.shape, 0)
    g_prev = jnp.where(rows == 0, halo_ref[0:1, :], pltpu.roll(g, 1, axis=0))
    g_next = jnp.where(rows == bm - 1, halo_ref[1:2, :], pltpu.roll(g, bm - 1, axis=0))
    y = g_prev * cw_ref[0:1, :] + g * cw_ref[1:2, :] + g_next * cw_ref[2:3, :] + cb_ref[...]
    v = jnp.dot(x, wv_ref[...], preferred_element_type=F32)
    o_ref[...] = (y * jax.nn.sigmoid(y) * v).astype(o_ref.dtype)


def _ffn_up(h2, lw):
    m, d = h2.shape
    dff = lw["conv_b"].shape[1]
    bm = _tile(m, 1024)
    bn = _tile(dff, 512, LANES)
    nt, nj = m // bm, dff // bn

    edge = h2.reshape(nt, bm, d)[:, (0, bm - 1), :].reshape(2 * nt, d)
    pad = (-edge.shape[0]) % 16
    edge = jnp.pad(edge, ((0, pad), (0, 0)))
    bne = _tile(dff, 1024, LANES)
    eg = _matmul(edge, lw["w_up"]._replace(n=dff), _epi_cast, [], [],
                 jax.ShapeDtypeStruct((edge.shape[0], dff), F32),
                 pl.BlockSpec((edge.shape[0], bne), lambda i, j: (i, j)),
                 edge.shape[0], bne, "ffn_edge_rows")
    eg = eg[:2 * nt].reshape(nt, 2, dff)
    zero = jnp.zeros((1, dff), F32)
    prev_rows = jnp.concatenate([zero, eg[:-1, 1]], axis=0)
    next_rows = jnp.concatenate([eg[1:, 0], zero], axis=0)
    halo = jnp.stack([prev_rows, next_rows], axis=1)

    return pl.pallas_call(
        _ffn_up_kernel,
        grid=(m // bm, nj),
        in_specs=[pl.BlockSpec((bm, d), lambda i, j: (i, 0)),
                  _w_spec(lw["w_up"], bn),
                  _w_spec(lw["w_up"], bn, shift=nj),
                  pl.BlockSpec((None, 2, bn), lambda i, j: (i, 0, j)),
                  pl.BlockSpec((CONV_W, bn), lambda i, j: (0, j)),
                  pl.BlockSpec((1, bn), lambda i, j: (0, j))],
        out_specs=pl.BlockSpec((bm, bn), lambda i, j: (i, j)),
        out_shape=jax.ShapeDtypeStruct((m, dff), BF16),
        compiler_params=_params("parallel", "parallel"),
        name="ffn_up_conv",
    )(h2, lw["w_up"].arr, lw["w_up"].arr, halo, lw["conv_w"], lw["conv_b"])


def _rope_partner():
    half = MLA_ROPE_DIM // 2
    quarter = half // 2
    idx = np.arange(MLA_ROPE_DIM)
    return np.where((idx % half) < quarter, idx + quarter, idx - quarter)


def _pad_lanes(a):
    return jnp.pad(a, [(0, 0)] * (a.ndim - 1) + [(0, LANES - a.shape[-1])])


def _stacked_weights(w_in, mla_w_q_up, mla_w_kv_up, w_branch_na, w_branch_mla, w_out, ffn_w_up, ffn_w_down):
    depth = w_in.shape[0]
    naw = NA_HEADS * NA_HEAD_DIM
    q_rank = mla_w_q_up.shape[1]
    kv_rank = mla_w_kv_up.shape[1]
    perm = _rope_partner()
    o_cq = 3 * naw
    o_ckv = o_cq + q_rank
    o_kr = o_ckv + kv_rank
    o_g = o_kr + MLA_ROPE_DIM
    w_in = w_in.astype(BF16)
    w_kr = w_in[:, :, o_kr:o_g]
    wq = mla_w_q_up.reshape(depth, q_rank, MLA_HEADS, MLA_QK_DIM)
    wq_rope = wq[..., MLA_NOPE_DIM:]
    return {
        "in_head": w_in,
        "ckv_kr": jnp.concatenate([w_in[:, :, o_ckv:o_kr], _pad_lanes(w_kr), _pad_lanes(w_kr[:, :, perm])], axis=2),
        "gates": w_in[:, :, o_g:],
        "q_up": jnp.concatenate([wq[..., :MLA_NOPE_DIM], _pad_lanes(wq_rope), _pad_lanes(wq_rope[..., perm])],
                                axis=3).reshape(depth, q_rank, MLA_HEADS * 3 * LANES).astype(BF16),
        "kv_up": mla_w_kv_up.astype(BF16),
        "b_na": w_branch_na.astype(BF16),
        "b_mla": w_branch_mla.astype(BF16),
        "out": w_out.astype(BF16),
        "up": ffn_w_up.astype(BF16),
        "down": ffn_w_down.astype(BF16),
    }


def _layer_weights(l, sw, na_q_norm, na_k_norm, mla_cq_norm, mla_ckv_norm, mla_q_norm, mla_k_norm,
                   ffn_conv_w, ffn_conv_b):
    naw = NA_HEADS * NA_HEAD_DIM
    q_rank = sw["q_up"].shape[1]
    kv_rank = sw["kv_up"].shape[1]
    perm = _rope_partner()
    whole = lambda a: _W(a, l, 0, a.shape[2])
    lw = {
        "w_na_qk": _W(sw["in_head"], l, 0, 2 * naw),
        "w_na_v": _W(sw["in_head"], l, 2 * naw, naw),
        "w_cq": _W(sw["in_head"], l, 3 * naw, q_rank),
        "w_ckv_kr": whole(sw["ckv_kr"]),
        "w_gates": whole(sw["gates"]),
        "w_q_up": whole(sw["q_up"]),
        "w_kv_up": whole(sw["kv_up"]),
        "w_b_na": whole(sw["b_na"]),
        "w_b_mla": whole(sw["b_mla"]),
        "w_out": whole(sw["out"]),
        "w_up": whole(sw["up"]),
        "w_down": whole(sw["down"]),
        "g_na_qk": jnp.concatenate([jnp.tile(na_q_norm[l] * (NA_HEAD_DIM ** -0.5 * LOG2E), NA_HEADS),
                                    jnp.tile(na_k_norm[l], NA_HEADS)]).reshape(1, 2 * naw),
        "g_cq": mla_cq_norm[l].reshape(1, q_rank),
        "g_ckv": mla_ckv_norm[l].reshape(1, kv_rank),
        "conv_w": ffn_conv_w[l],
        "conv_b": ffn_conv_b[l].reshape(1, -1),
    }
    for tag, g in (("q", mla_q_norm[l]), ("k", mla_k_norm[l])):
        g_rope = g[MLA_NOPE_DIM:]
        lw[f"g{tag}_nope"] = g[:MLA_NOPE_DIM].reshape(1, LANES)
        lw[f"g{tag}_rope"] = _pad_lanes(g_rope).reshape(1, LANES)
        lw[f"g{tag}_swap"] = _pad_lanes(g_rope[perm]).reshape(1, LANES)
    return lw


def _rope_tables(n_tok):
    t = jnp.arange(n_tok, dtype=jnp.int32)
    half = MLA_ROPE_DIM // 2
    inv_freq = ROPE_THETA ** (-jnp.arange(0, half, 2, dtype=F32) / half)
    cs, sn = [], []
    for pos in (t // GRID_W, t % GRID_W):
        ang = pos.astype(F32)[:, None] * inv_freq[None, :]
        cs += [jnp.cos(ang), jnp.cos(ang)]
        sn += [-jnp.sin(ang), jnp.sin(ang)]
    return _pad_lanes(jnp.concatenate(cs, axis=1)), _pad_lanes(jnp.concatenate(sn, axis=1))


def _identity_tables(n_tok):
    cos_t = _pad_lanes(jnp.ones((n_tok, MLA_ROPE_DIM), F32))
    return cos_t, jnp.zeros((n_tok, LANES), F32)


def _head_major(a, heads):
    rows = a.shape[0]
    return a.reshape(rows, heads, -1).transpose(1, 0, 2)


def _head_major_vt(v, heads):
    rows = v.shape[0]
    vt = v.reshape(rows, heads, MLA_V_DIM).transpose(1, 2, 0)
    tail = jnp.zeros((heads, BF16_SUBLANES, rows), v.dtype).at[:, 0, :].set(1.0)
    return jnp.concatenate([vt, tail], axis=1)


def kernel(x, c, ctx, c_ctx, ada_down, ada_up, ada_bias, norm_mix, norm_ffn, w_in, na_q_norm, na_k_norm, na_rpb, mla_cq_norm, mla_ckv_norm, mla_w_q_up, mla_w_kv_up, mla_q_norm, mla_k_norm, w_branch_na, w_branch_mla, w_out, ffn_w_up, ffn_conv_w, ffn_conv_b, ffn_w_down):
    batch, n_tok, d = x.shape
    assert batch == 1 and c.shape[0] == 1
    assert n_tok % GRID_W == 0 and NA_HEADS % NA_GROUP == 0
    depth = w_in.shape[0]
    naw = NA_HEADS * NA_HEAD_DIM
    xs = x[0]
    cs = ctx[0]
    lat_tabs = _rope_tables(n_tok)
    ctx_tabs = _identity_tables(cs.shape[0])
    cond = jnp.zeros((8, d), F32).at[0].set(c[0]).at[1].set(c_ctx)
    sw = _stacked_weights(w_in, mla_w_q_up, mla_w_kv_up, w_branch_na, w_branch_mla, w_out, ffn_w_up, ffn_w_down)

    for l in range(depth):
        last = l == depth - 1
        lw = _layer_weights(l, sw, na_q_norm, na_k_norm, mla_cq_norm, mla_ckv_norm, mla_q_norm, mla_k_norm,
                            ffn_conv_w, ffn_conv_b)
        mods = _adaln(cond, ada_down, ada_up, ada_bias, l)
        sh1, sc1, g1, sh2, sc2, g2 = [mods[0, i * d:(i + 1) * d] for i in range(N_MOD)]
        csh1, csc1, cg1, csh2, csc2, cg2 = [mods[1, i * d:(i + 1) * d] for i in range(N_MOD)]

        h = _norm_mod(xs, norm_mix[l], sc1, sh1)
        hc = _norm_mod(cs, norm_mix[l], csc1, csh1)
        na_qk, na_v, mq, mk, mv = _project(h, lw, lat_tabs)
        c_qk, c_v, cmq, cmk, cmv = _project(hc, lw, ctx_tabs)

        y_na = _na_attention(na_qk, na_v, c_qk, c_v, _na_bias_table(na_rpb[l], n_tok // GRID_W, cs.shape[0]))
        y_mla = _flash(mq, mk, mv, (cmk, cmv), name="mla_attention")
        u = _merge(h, y_na, y_mla, lw)
        xs = _residual_matmul(u, lw["w_out"], xs, g1, "out_proj")
        h2 = _norm_mod(xs, norm_ffn[l], sc2, sh2)
        xs = _residual_matmul(_ffn_up(h2, lw), lw["w_down"], xs, g2, "ffn_down")

        if not last:
            yc_na = _flash(_head_major(c_qk[:, :naw], NA_HEADS), _head_major(c_qk[:, naw:], NA_HEADS),
                           _head_major_vt(c_v, NA_HEADS), None, name="ctx_na_attention")
            yc_mla = _flash(cmq, cmk, cmv, None, name="ctx_mla_attention")
            uc = _merge(hc, yc_na, yc_mla, lw)
            cs = _residual_matmul(uc, lw["w_out"], cs, cg1, "ctx_out_proj")
            hc2 = _norm_mod(cs, norm_ffn[l], csc2, csh2)
            cs = _residual_matmul(_ffn_up(hc2, lw), lw["w_down"], cs, cg2, "ctx_ffn_down")
    return xs[None]
```
